```python
import math
import jax, jax.numpy as jnp
from jax import lax
import numpy as np

D_MODEL = 1024
BATCH = 16
SEQ = 4096
DEPTH = 4

HEAD_DIM = 64
RMS_EPS = 1e-6
A_WIDTH = D_MODEL // 2
A_HEADS = A_WIDTH // HEAD_DIM
DECAY_LORA = 64
AAA_LORA = 64
GATE_LORA = 128
GN_EPS = 64e-5
B_WIDTH = D_MODEL // 2
B_GROUPS = 4
B_GROUP_DIM = B_WIDTH // B_GROUPS
CHUNK = 128
LN_EPS = 1e-5
OFF_K = A_WIDTH
OFF_V = 2 * A_WIDTH
OFF_W = 3 * A_WIDTH
OFF_A = OFF_W + DECAY_LORA
OFF_G = OFF_A + AAA_LORA
A_COLS = OFF_G + GATE_LORA
IN_COLS = A_COLS + 2 * B_WIDTH
Q_HEADS = D_MODEL // HEAD_DIM
KV_HEADS = 4
Q_PER_KV = Q_HEADS // KV_HEADS
WINDOW = 128
ROPE_THETA = 10000.0
QKV_COLS = (Q_HEADS + 2 * KV_HEADS) * HEAD_DIM
D_FF = 2816
N_EVEN = (DEPTH + 1) // 2
N_ODD = DEPTH // 2

kernel_name = "hybrid_rwkv7_gmlp_swa_macaron"


def rms_norm(x, gain, eps=RMS_EPS):
    xf = x.astype(jnp.float32)
    y = xf * lax.rsqrt(jnp.mean(xf * xf, axis=-1, keepdims=True) + eps)
    return (y * gain.astype(jnp.float32)).astype(x.dtype)


def layer_norm(x, gain, bias, eps=LN_EPS):
    xf = x.astype(jnp.float32)
    mu = jnp.mean(xf, axis=-1, keepdims=True)
    var = jnp.mean(jnp.square(xf - mu), axis=-1, keepdims=True)
    return ((xf - mu) * lax.rsqrt(var + eps) * gain.astype(jnp.float32) + bias.astype(jnp.float32)).astype(x.dtype)


def swiglu_ffn(x, w_gate, w_up, w_down):
    return (jax.nn.silu(x @ w_gate) * (x @ w_up)) @ w_down


def token_shift(z):
    return jnp.pad(z, ((0, 0), (1, 0), (0, 0)))[:, :-1]


def rope(x, cos, sin):
    x1, x2 = jnp.split(x, 2, axis=-1)
    return jnp.concatenate([x1 * cos - x2 * sin, x2 * cos + x1 * sin], axis=-1).astype(x.dtype)


def rwkv7_scan(r, w, k, v, a, b):
    def step(S, inp):
        r_t, w_t, k_t, v_t, a_t, b_t = inp
        sa = jnp.einsum('bhij,bhj->bhi', S, a_t)
        S = S * w_t[:, :, None, :] + sa[..., None] * b_t[:, :, None, :] + v_t[..., None] * k_t[:, :, None, :]
        return S, jnp.einsum('bhij,bhj->bhi', S, r_t)
    xs = tuple(jnp.moveaxis(t.astype(jnp.float32), 1, 0) for t in (r, w, k, v, a, b))
    bsz, _, heads, dim = r.shape
    s0 = jnp.zeros((bsz, heads, dim, dim), jnp.float32)
    _, y = lax.scan(step, s0, xs)
    return jnp.moveaxis(y, 0, 1)


def rwkv_gmlp_mixer(h, w_in, mu_shift, w0, w_decay_up, a0, w_aaa_up, w_gate_up, k_k, k_a, r_k,
                    gn_gain, gn_bias, sg_ln_gain, sg_ln_bias, w_spatial, b_spatial, w_out):
    bsz, t_len, _ = h.shape
    z = h @ w_in
    z_a, z_b = z[..., :A_COLS], z[..., A_COLS:]

    z_a = z_a + (token_shift(z_a) - z_a) * mu_shift
    r, k, v, xw, xa, xg = jnp.split(z_a, [OFF_K, OFF_V, OFF_W, OFF_A, OFF_G], axis=-1)
    w_log = -jax.nn.softplus(-(w0 + jnp.tanh(xw) @ w_decay_up)) - 0.5
    decay = jnp.exp(-jnp.exp(w_log.astype(jnp.float32)))
    iclr = jax.nn.sigmoid(a0 + xa @ w_aaa_up)
    gate = jax.nn.sigmoid(xg) @ w_gate_up
    heads = lambda t: t.reshape(bsz, t_len, A_HEADS, HEAD_DIM)
    kk = heads(k * k_k).astype(jnp.float32)
    kk = kk / jnp.maximum(jnp.sqrt(jnp.sum(kk * kk, axis=-1, keepdims=True)), 1e-12)
    k = k * (1.0 + (iclr - 1.0) * k_a)
    r_h, k_h, v_h, a_h = heads(r), heads(k), heads(v), heads(iclr)
    y = rwkv7_scan(r_h, heads(decay), k_h, v_h, -kk, kk * a_h.astype(jnp.float32))
    mu = jnp.mean(y, axis=-1, keepdims=True)
    var = jnp.mean(jnp.square(y - mu), axis=-1, keepdims=True)
    y = ((y - mu) * lax.rsqrt(var + GN_EPS)).reshape(bsz, t_len, A_WIDTH)
    y = (y * gn_gain.astype(jnp.float32) + gn_bias.astype(jnp.float32)).astype(h.dtype)
    bonus = (jnp.sum(r_h * k_h * r_k, axis=-1, keepdims=True) * v_h).reshape(bsz, t_len, A_WIDTH)
    y_a = (y + bonus) * gate

    z_b = jax.nn.gelu(z_b, approximate=False)
    u, s = jnp.split(z_b, 2, axis=-1)
    s = layer_norm(s, sg_ln_gain, sg_ln_bias)
    s = s.reshape(bsz, t_len // CHUNK, CHUNK, B_GROUPS, B_GROUP_DIM)
    causal = jnp.tril(jnp.ones((CHUNK, CHUNK), dtype=bool))
    w_s = jnp.where(causal[None], w_spatial, jnp.zeros((), w_spatial.dtype))
    s = jnp.einsum('gts,bcsgd->bctgd', w_s, s) + b_spatial.T[None, None, :, :, None]
    y_b = u * s.reshape(bsz, t_len, B_WIDTH)

    return jnp.concatenate([y_a, y_b], axis=-1) @ w_out


def swa_sink_attention(h, cos, sin, w_qkv, b_qkv, q_norm, k_norm, sinks, w_o, b_o):
    bsz, t_len, _ = h.shape
    nb = t_len // WINDOW
    qkv = h @ w_qkv + b_qkv
    q, k, v = jnp.split(qkv, [Q_HEADS * HEAD_DIM, (Q_HEADS + KV_HEADS) * HEAD_DIM], axis=-1)
    q = rope(rms_norm(q.reshape(bsz, t_len, Q_HEADS, HEAD_DIM), q_norm), cos, sin)
    k = rope(rms_norm(k.reshape(bsz, t_len, KV_HEADS, HEAD_DIM), k_norm), cos, sin)
    v = v.reshape(bsz, t_len, KV_HEADS, HEAD_DIM)

    q_blk = q.reshape(bsz, nb, WINDOW, KV_HEADS, Q_PER_KV, HEAD_DIM).transpose(1, 0, 2, 3, 4, 5)

    def band(t):
        tb = t.reshape(bsz, nb, WINDOW, KV_HEADS, HEAD_DIM)
        prev = jnp.pad(tb, ((0, 0), (1, 0), (0, 0), (0, 0), (0, 0)))[:, :-1]
        return jnp.concatenate([prev, tb], axis=2).transpose(1, 0, 2, 3, 4)

    k_band, v_band = band(k), band(v)
    qi = jnp.arange(WINDOW)[:, None]
    kj = jnp.arange(2 * WINDOW)[None, :]
    dist = qi + WINDOW - kj
    in_window = (dist >= 0) & (dist < WINDOW)
    scale = HEAD_DIM ** -0.5
    sink = sinks.astype(jnp.float32).reshape(KV_HEADS, Q_PER_KV)[None, :, :, None]

    def block(args):
        n, qb, kb, vb = args
        s = jnp.einsum('bqhgd,bkhd->bhgqk', qb, kb, preferred_element_type=jnp.float32) * scale
        valid = in_window & ((n > 0) | (kj >= WINDOW))
        s = jnp.where(valid, s, -jnp.inf)
        m = jnp.maximum(jnp.max(s, axis=-1), sink)
        p = jnp.exp(s - m[..., None])
        denom = jnp.sum(p, axis=-1) + jnp.exp(sink - m)
        o = jnp.einsum('bhgqk,bkhd->bqhgd', p, vb.astype(jnp.float32))
        return (o / denom.transpose(0, 3, 1, 2)[..., None]).astype(qb.dtype)

    o = lax.map(block, (jnp.arange(nb), q_blk, k_band, v_band))
    o = o.transpose(1, 0, 2, 3, 4, 5).reshape(bsz, t_len, Q_HEADS * HEAD_DIM)
    return o @ w_o + b_o


def setup_inputs(seed: int = 0) -> dict:
    key = jax.random.key(seed)
    ks = iter(jax.random.split(key, 40))
    nrm = lambda shape, scale: scale * jax.random.normal(next(ks), shape, jnp.float32)
    d = D_MODEL
    inp = {}
    inp["x"] = nrm((BATCH, SEQ, d), 1.0)
    inp["positions"] = jnp.broadcast_to(jnp.arange(SEQ, dtype=jnp.int32)[None, :], (BATCH, SEQ)).astype(jnp.int32)
    inp["ffn1_norm"] = 1.0 + nrm((DEPTH, d), 0.1)
    inp["ffn1_w_gate"] = nrm((DEPTH, d, D_FF), d ** -0.5)
    inp["ffn1_w_up"] = nrm((DEPTH, d, D_FF), d ** -0.5)
    inp["ffn1_w_down"] = nrm((DEPTH, D_FF, d), D_FF ** -0.5)
    inp["mix_norm"] = 1.0 + nrm((DEPTH, d), 0.1)
    inp["ffn2_norm"] = 1.0 + nrm((DEPTH, d), 0.1)
    inp["ffn2_w_gate"] = nrm((DEPTH, d, D_FF), d ** -0.5)
    inp["ffn2_w_up"] = nrm((DEPTH, d, D_FF), d ** -0.5)
    inp["ffn2_w_down"] = nrm((DEPTH, D_FF, d), D_FF ** -0.5)
    inp["ab_w_in"] = nrm((N_EVEN, d, IN_COLS), d ** -0.5)
    inp["ab_mu_shift"] = jax.random.uniform(next(ks), (N_EVEN, A_COLS), jnp.float32)
    inp["rwkv_w0"] = -2.0 + nrm((N_EVEN, A_WIDTH), 0.7)
    inp["rwkv_w_decay_up"] = nrm((N_EVEN, DECAY_LORA, A_WIDTH), 0.5 * DECAY_LORA ** -0.5)
    inp["rwkv_a0"] = nrm((N_EVEN, A_WIDTH), 0.1)
    inp["rwkv_w_aaa_up"] = nrm((N_EVEN, AAA_LORA, A_WIDTH), 0.5 * AAA_LORA ** -0.5)
    inp["rwkv_w_gate_up"] = nrm((N_EVEN, GATE_LORA, A_WIDTH), GATE_LORA ** -0.5)
    inp["rwkv_k_k"] = 0.85 + nrm((N_EVEN, A_WIDTH), 0.1)
    inp["rwkv_k_a"] = 1.0 + nrm((N_EVEN, A_WIDTH), 0.1)
    inp["rwkv_r_k"] = nrm((N_EVEN, A_HEADS, HEAD_DIM), 0.1)
    inp["rwkv_gn_gain"] = 1.0 + nrm((N_EVEN, A_WIDTH), 0.1)
    inp["rwkv_gn_bias"] = nrm((N_EVEN, A_WIDTH), 0.02)
    inp["sg_ln_gain"] = 1.0 + nrm((N_EVEN, B_WIDTH), 0.1)
    inp["sg_ln_bias"] = nrm((N_EVEN, B_WIDTH), 0.02)
    inp["sg_w_spatial"] = nrm((N_EVEN, B_GROUPS, CHUNK, CHUNK), 0.5 * CHUNK ** -0.5)
    inp["sg_b_spatial"] = 1.0 + nrm((N_EVEN, B_GROUPS, CHUNK), 0.1)
    inp["ab_w_out"] = nrm((N_EVEN, A_WIDTH + B_WIDTH, d), (A_WIDTH + B_WIDTH) ** -0.5)
    inp["attn_w_qkv"] = nrm((N_ODD, d, QKV_COLS), d ** -0.5)
    inp["attn_b_qkv"] = nrm((N_ODD, QKV_COLS), 0.02)
    inp["attn_q_norm"] = 1.0 + nrm((N_ODD, HEAD_DIM), 0.1)
    inp["attn_k_norm"] = 1.0 + nrm((N_ODD, HEAD_DIM), 0.1)
    inp["attn_sinks"] = nrm((N_ODD, Q_HEADS), 1.0)
    inp["attn_w_o"] = nrm((N_ODD, Q_HEADS * HEAD_DIM, d), (Q_HEADS * HEAD_DIM) ** -0.5)
    inp["attn_b_o"] = nrm((N_ODD, d), 0.02)
    return inp


def reference(x, positions, ffn1_norm, ffn1_w_gate, ffn1_w_up, ffn1_w_down, mix_norm,
              ffn2_norm, ffn2_w_gate, ffn2_w_up, ffn2_w_down,
              ab_w_in, ab_mu_shift, rwkv_w0, rwkv_w_decay_up, rwkv_a0, rwkv_w_aaa_up, rwkv_w_gate_up,
              rwkv_k_k, rwkv_k_a, rwkv_r_k, rwkv_gn_gain, rwkv_gn_bias,
              sg_ln_gain, sg_ln_bias, sg_w_spatial, sg_b_spatial, ab_w_out,
              attn_w_qkv, attn_b_qkv, attn_q_norm, attn_k_norm, attn_sinks, attn_w_o, attn_b_o):
    inv_freq = ROPE_THETA ** (-jnp.arange(0, HEAD_DIM, 2, dtype=jnp.float32) / HEAD_DIM)
    ang = positions.astype(jnp.float32)[..., None] * inv_freq
    cos = jnp.cos(ang)[:, :, None, :]
    sin = jnp.sin(ang)[:, :, None, :]
    for l in range(DEPTH):
        h = rms_norm(x, ffn1_norm[l])
        x = x + 0.5 * swiglu_ffn(h, ffn1_w_gate[l], ffn1_w_up[l], ffn1_w_down[l])
        h = rms_norm(x, mix_norm[l])
        i = l // 2
        if l % 2 == 0:
            x = x + rwkv_gmlp_mixer(h, ab_w_in[i], ab_mu_shift[i], rwkv_w0[i], rwkv_w_decay_up[i],
                                    rwkv_a0[i], rwkv_w_aaa_up[i], rwkv_w_gate_up[i], rwkv_k_k[i],
                                    rwkv_k_a[i], rwkv_r_k[i], rwkv_gn_gain[i], rwkv_gn_bias[i],
                                    sg_ln_gain[i], sg_ln_bias[i], sg_w_spatial[i], sg_b_spatial[i],
                                    ab_w_out[i])
        else:
            x = x + swa_sink_attention(h, cos, sin, attn_w_qkv[i], attn_b_qkv[i], attn_q_norm[i],
                                       attn_k_norm[i], attn_sinks[i], attn_w_o[i], attn_b_o[i])
        h = rms_norm(x, ffn2_norm[l])
        x = x + 0.5 * swiglu_ffn(h, ffn2_w_gate[l], ffn2_w_up[l], ffn2_w_down[l])
    return x
```

```python
import functools
import math

import jax
import jax.numpy as jnp
from jax import lax
from jax.experimental import pallas as pl
from jax.experimental.pallas import tpu as pltpu

F32 = jnp.float32
BF16 = jnp.bfloat16

HEAD_DIM = 64
RMS_EPS = 1e-6
GN_EPS = 64e-5
LN_EPS = 1e-5
ROPE_THETA = 10000.0
WINDOW = 128
SCAN_CHUNK = 64
LANES = 128
VMEM_LIMIT = 56 * 1024 * 1024


def _cparams(*sem):
    return pltpu.CompilerParams(dimension_semantics=sem, vmem_limit_bytes=VMEM_LIMIT)


def _resident(shape):
    nd = len(shape)
    return pl.BlockSpec(shape, lambda *_: (0,) * nd, pipeline_mode=pl.Buffered(1))


def _dot(a, b):
    return jnp.dot(a, b, preferred_element_type=F32)


def _dot_nt(a, b):
    return lax.dot_general(a, b, (((1,), (1,)), ((), ())), preferred_element_type=F32)


def _dot_tn(a, b):
    return lax.dot_general(a, b, (((0,), (0,)), ((), ())), preferred_element_type=F32)


def _rms(x, gain):
    return x * lax.rsqrt(jnp.mean(x * x, axis=-1, keepdims=True) + RMS_EPS) * gain


def _split_bf16(x):
    hi = x.astype(BF16)
    lo = (x - hi.astype(F32)).astype(BF16)
    return hi, lo


def _group_sum(x, ones_bd):
    hi, lo = _split_bf16(x)
    return _dot(hi, ones_bd) + _dot(lo, ones_bd)


def _ffn_body(x_ref, gain_ref, wg_ref, wu_ref, wd_ref, o_ref, *, n_chunks):
    x = x_ref[...]
    h = _rms(x, gain_ref[...]).astype(BF16)
    fc = wg_ref.shape[1] // n_chunks
    acc = jnp.zeros_like(x)
    for c in range(n_chunks):
        g = _dot(h, wg_ref[:, c * fc:(c + 1) * fc])
        u = _dot(h, wu_ref[:, c * fc:(c + 1) * fc])
        act = (g * jax.nn.sigmoid(g) * u).astype(BF16)
        acc = acc + _dot(act, wd_ref[c * fc:(c + 1) * fc, :])
    o_ref[...] = x + 0.5 * acc


def _ffn(x, gain, wg, wu, wd, *, tm=512, n_chunks=2):
    n, d = x.shape
    f = wg.shape[1]
    return pl.pallas_call(
        functools.partial(_ffn_body, n_chunks=n_chunks),
        grid=(n // tm,),
        in_specs=[pl.BlockSpec((tm, d), lambda i: (i, 0)),
                  _resident((1, d)), _resident((d, f)), _resident((d, f)), _resident((f, d))],
        out_specs=pl.BlockSpec((tm, d), lambda i: (i, 0)),
        out_shape=jax.ShapeDtypeStruct((n, d), F32),
        compiler_params=_cparams("parallel"),
        name="ffn",
    )(x, gain, wg, wu, wd)


def _rope_body(pos_ref, freq_ref, cos_ref, sin_ref):
    ang = pos_ref[...].astype(F32) * freq_ref[...]
    lane = lax.broadcasted_iota(jnp.int32, ang.shape, 1)
    first_half = (lane % HEAD_DIM) < HEAD_DIM // 2
    cos_ref[...] = jnp.cos(ang)
    s = jnp.sin(ang)
    sin_ref[...] = jnp.where(first_half, -s, s)


def _rope_tables(pos, *, tm=2048):
    n = pos.shape[0]
    tm = min(tm, n)
    half = HEAD_DIM // 2
    inv_freq = ROPE_THETA ** (-jnp.arange(0, HEAD_DIM, 2, dtype=F32) / HEAD_DIM)
    freq = jnp.tile(inv_freq, LANES // half)[None, :]
    return pl.pallas_call(
        _rope_body,
        grid=(n // tm,),
        in_specs=[pl.BlockSpec((tm, 1), lambda i: (i, 0)), _resident((1, LANES))],
        out_specs=[pl.BlockSpec((tm, LANES), lambda i: (i, 0))] * 2,
        out_shape=[jax.ShapeDtypeStruct((n, LANES), F32)] * 2,
        compiler_params=_cparams("parallel"),
        name="rope_tables",
    )(pos, freq)


def _attn_body(sink_ref, x_ref, cos_ref, sin_ref, gain_ref, wqkv_ref, bqkv_ref, qg_ref, kg_ref, wo_ref, bo_ref,
               o_ref, k2_prev, v2_prev, *, n_q, n_kv):
    t = pl.program_id(1)
    tq = x_ref.shape[0]
    nblk = tq // WINDOW
    qw = n_q * HEAD_DIM
    kw = n_kv * HEAD_DIM
    per_kv = n_q // n_kv

    @pl.when(t == 0)
    def _():
        k2_prev[...] = jnp.zeros_like(k2_prev)
        v2_prev[...] = jnp.zeros_like(v2_prev)

    x = x_ref[...]
    h = _rms(x, gain_ref[...]).astype(BF16)
    qkv = _dot(h, wqkv_ref[...]) + bqkv_ref[...]

    lane = lax.broadcasted_iota(jnp.int32, (tq, LANES), 1)
    lo = lane < HEAD_DIM
    first_half = (lane % HEAD_DIM) < HEAD_DIM // 2
    cos = cos_ref[...]
    sin = sin_ref[...]

    def norm_rope(z, gain_pair, extra):
        zz = z * z
        s_lo = jnp.sum(jnp.where(lo, zz, 0.0), axis=-1, keepdims=True)
        s_all = jnp.sum(zz, axis=-1, keepdims=True)
        inv_lo = lax.rsqrt(s_lo * (1.0 / HEAD_DIM) + RMS_EPS)
        inv_hi = lax.rsqrt((s_all - s_lo) * (1.0 / HEAD_DIM) + RMS_EPS)
        zg = z * gain_pair
        partner = jnp.where(first_half, pltpu.roll(zg, LANES - HEAD_DIM // 2, axis=1),
                            pltpu.roll(zg, HEAD_DIM // 2, axis=1))
        rot = zg * cos + partner * sin
        return rot * (jnp.where(lo, inv_lo, inv_hi) * extra)

    scale = HEAD_DIM ** -0.5
    q_pairs = [norm_rope(qkv[:, p * LANES:(p + 1) * LANES], qg_ref[...], scale).astype(BF16)
               for p in range(qw // LANES)]
    k_pairs = [norm_rope(qkv[:, qw + p * LANES: qw + (p + 1) * LANES], kg_ref[...], 1.0)
               for p in range(kw // LANES)]
    v_pairs = [qkv[:, qw + kw + p * LANES: qw + kw + (p + 1) * LANES] for p in range(kw // LANES)]

    k2, v_top, v_bot = [], [], []
    for g in range(n_kv):
        kp = k_pairs[g // 2]
        vp = v_pairs[g // 2]
        kr = pltpu.roll(kp, HEAD_DIM, axis=1)
        vr = pltpu.roll(vp, HEAD_DIM, axis=1)
        if g % 2 == 0:
            k2.append(jnp.where(lo, kp, kr).astype(BF16))
            v_top.append(jnp.where(lo, vp, 0.0).astype(BF16))
            v_bot.append(jnp.where(lo, 0.0, vr).astype(BF16))
        else:
            k2.append(jnp.where(lo, kr, kp).astype(BF16))
            v_top.append(jnp.where(lo, vr, 0.0).astype(BF16))
            v_bot.append(jnp.where(lo, 0.0, vp).astype(BF16))

    qi = lax.broadcasted_iota(jnp.int32, (WINDOW, 2 * WINDOW), 0)
    kj = lax.broadcasted_iota(jnp.int32, (WINDOW, 2 * WINDOW), 1)
    dist = qi + WINDOW - kj
    in_window = (dist >= 0) & (dist < WINDOW)
    lo_blk = lax.broadcasted_iota(jnp.int32, (WINDOW, LANES), 1) < HEAD_DIM

    o_rows = []
    for blk in range(nblk):
        rows = slice(blk * WINDOW, (blk + 1) * WINDOW)
        if blk == 0:
            valid = in_window & (kj >= jnp.where(t > 0, 0, WINDOW))
        else:
            valid = in_window
        valid2 = jnp.concatenate([valid, valid], axis=0)
        o_pairs = []
        for g in range(n_kv):
            if blk == 0:
                kprev, vtp, vbp = k2_prev[g], v2_prev[g, 0], v2_prev[g, 1]
            else:
                prows = slice((blk - 1) * WINDOW, blk * WINDOW)
                kprev, vtp, vbp = k2[g][prows], v_top[g][prows], v_bot[g][prows]
            kband = jnp.concatenate([kprev, k2[g][rows]], axis=0)
            vband = jnp.concatenate([vtp, v_top[g][rows], vbp, v_bot[g][rows]], axis=0)
            for pp in range(per_kv // 2):
                pair = g * (per_kv // 2) + pp
                qp = q_pairs[pair][rows]
                zero = jnp.zeros_like(qp)
                lhs = jnp.concatenate([jnp.where(lo_blk, qp, zero), jnp.where(lo_blk, zero, qp)], axis=0)
                s = _dot_nt(lhs, kband)
                s = jnp.where(valid2, s, -jnp.inf)
                row2 = lax.broadcasted_iota(jnp.int32, (2 * WINDOW, 1), 0)
                sink = jnp.where(row2 < WINDOW, sink_ref[2 * pair], sink_ref[2 * pair + 1])
                m = jnp.maximum(jnp.max(s, axis=-1, keepdims=True), sink)
                p = jnp.exp(s - m)
                denom = jnp.sum(p, axis=-1, keepdims=True) + jnp.exp(sink - m)
                pb = p.astype(BF16)
                p_cat = jnp.concatenate([pb[:WINDOW], pb[WINDOW:]], axis=1)
                o = _dot(p_cat, vband)
                inv = 1.0 / denom
                o_pairs.append(o * jnp.where(lo_blk, inv[:WINDOW], inv[WINDOW:]))
        o_rows.append(jnp.concatenate(o_pairs, axis=1).astype(BF16))
    o_all = jnp.concatenate(o_rows, axis=0)

    last = slice((nblk - 1) * WINDOW, nblk * WINDOW)
    for g in range(n_kv):
        k2_prev[g] = k2[g][last]
        v2_prev[g, 0] = v_top[g][last]
        v2_prev[g, 1] = v_bot[g][last]

    o_ref[...] = x + _dot(o_all, wo_ref[...]) + bo_ref[...]


def _attn(x, cos, sin, gain, wqkv, bqkv, qg, kg, sinks, wo, bo, *, bsz, tq=256):
    n, d = x.shape
    t_len = n // bsz
    nt = t_len // tq
    n_q = wo.shape[0] // HEAD_DIM
    n_kv = (wqkv.shape[1] - wo.shape[0]) // (2 * HEAD_DIM)
    row = lambda b, t, *_: (b * nt + t, 0)
    grid_spec = pltpu.PrefetchScalarGridSpec(
        num_scalar_prefetch=1,
        grid=(bsz, nt),
        in_specs=[pl.BlockSpec((tq, d), row), pl.BlockSpec((tq, LANES), row), pl.BlockSpec((tq, LANES), row),
                  _resident((1, d)), _resident(wqkv.shape), _resident(bqkv.shape),
                  _resident((1, LANES)), _resident((1, LANES)), _resident(wo.shape), _resident(bo.shape)],
        out_specs=pl.BlockSpec((tq, d), row),
        scratch_shapes=[pltpu.VMEM((n_kv, WINDOW, LANES), BF16), pltpu.VMEM((n_kv, 2, WINDOW, LANES), BF16)],
    )
    return pl.pallas_call(
        functools.partial(_attn_body, n_q=n_q, n_kv=n_kv),
        grid_spec=grid_spec,
        out_shape=jax.ShapeDtypeStruct((n, d), F32),
        compiler_params=_cparams("parallel", "arbitrary"),
        name="swa_attention",
    )(sinks, x, cos, sin, gain, wqkv, bqkv, qg, kg, wo, bo)


def _mix_in_body(x_ref, xp_ref, gain_ref, win_ref, mu_ref, w0_ref, wdec_ref, a0_ref, waaa_ref, wgate_ref,
                 kk_ref, ka_ref, rk_ref, lng_ref, lnb_ref, wsp_ref, bsp_ref, ones_ref,
                 r_o, lw_o, k_o, v_o, a_o, b_o, gate_o, bonus_o, yb_o, *, a_width, b_width):
    t = pl.program_id(1)
    tm = x_ref.shape[0]
    a_cols = mu_ref.shape[1]
    gain = gain_ref[...]
    h = _rms(x_ref[...], gain).astype(BF16)
    hp = _rms(xp_ref[...], gain).astype(BF16)
    z = _dot(h, win_ref[...])
    zp = _dot(hp, win_ref[:, :a_cols])
    za = z[:, :a_cols]
    prev_row = jnp.where(t > 0, zp[xp_ref.shape[0] - 1:, :], 0.0)
    row = lax.broadcasted_iota(jnp.int32, (tm, 1), 0)
    shifted = jnp.where(row == 0, prev_row, pltpu.roll(za, 1, axis=0))
    za = za + (shifted - za) * mu_ref[...]

    aw = a_width
    r = za[:, :aw]
    k = za[:, aw:2 * aw]
    v = za[:, 2 * aw:3 * aw]
    xwa = za[:, 3 * aw:3 * aw + LANES]
    xg = za[:, 3 * aw + LANES:]
    y_dec = w0_ref[...] + _dot(jnp.tanh(xwa).astype(BF16), wdec_ref[...])
    lw = (-math.exp(-0.5)) * jax.nn.sigmoid(y_dec)
    iclr = jax.nn.sigmoid(a0_ref[...] + _dot(xwa.astype(BF16), waaa_ref[...]))
    gate = _dot(jax.nn.sigmoid(xg).astype(BF16), wgate_ref[...])
    ones_bd = ones_ref[...]
    kk = k * kk_ref[...]
    kk = kk / jnp.maximum(jnp.sqrt(_group_sum(kk * kk, ones_bd)), 1e-12)
    k = k * (1.0 + (iclr - 1.0) * ka_ref[...])
    r_o[...] = r
    lw_o[...] = lw
    k_o[...] = k
    v_o[...] = v
    a_o[...] = -kk
    b_o[...] = kk * iclr
    gate_o[...] = gate
    bonus_o[...] = _group_sum(r * k * rk_ref[...], ones_bd) * v

    zb = z[:, a_cols:]
    zb = 0.5 * zb * (1.0 + lax.erf(zb * math.sqrt(0.5)))
    u = zb[:, :b_width]
    s = zb[:, b_width:]
    mean = jnp.mean(s, axis=-1, keepdims=True)
    sc = s - mean
    var = jnp.mean(sc * sc, axis=-1, keepdims=True)
    s = (sc * lax.rsqrt(var + LN_EPS) * lng_ref[...] + lnb_ref[...]).astype(BF16)
    ti = lax.broadcasted_iota(jnp.int32, (WINDOW, WINDOW), 0)
    si = lax.broadcasted_iota(jnp.int32, (WINDOW, WINDOW), 1)
    n_groups = wsp_ref.shape[0]
    gd = b_width // n_groups
    for g in range(n_groups):
        w_g = jnp.where(si <= ti, wsp_ref[g], 0.0).astype(BF16)
        for c in range(tm // WINDOW):
            rows = slice(c * WINDOW, (c + 1) * WINDOW)
            cols = slice(g * gd, (g + 1) * gd)
            mixed = _dot(w_g, s[rows, cols]) + bsp_ref[:, cols]
            yb_o[rows, cols] = u[rows, cols] * mixed


def _mix_in(x, gain, win, mu, w0, wdec, a0, waaa, wgate, kk, ka, rk, lng, lnb, wsp, bsp, ones_bd, *, bsz, tm=256):
    n, d = x.shape
    t_len = n // bsz
    nt = t_len // tm
    a_width = w0.shape[1]
    b_width = lng.shape[1]
    prev_rows = 8
    row = lambda b, t: (b * nt + t, 0)
    prev = lambda b, t: (jnp.maximum((b * nt + t) * (tm // prev_rows) - 1, 0), 0)
    consts = [gain, win, mu, w0, wdec, a0, waaa, wgate, kk, ka, rk, lng, lnb, wsp, bsp, ones_bd]
    out_spec = pl.BlockSpec((tm, a_width), row)
    return pl.pallas_call(
        functools.partial(_mix_in_body, a_width=a_width, b_width=b_width),
        grid=(bsz, nt),
        in_specs=[pl.BlockSpec((tm, d), row), pl.BlockSpec((prev_rows, d), prev)] + [_resident(c.shape) for c in consts],
        out_specs=[out_spec] * 9,
        out_shape=[jax.ShapeDtypeStruct((n, a_width), F32)] * 9,
        compiler_params=_cparams("parallel", "parallel"),
        name="mix_in",
    )(x, x, *consts)


def _scan_body(r_ref, lw_ref, k_ref, v_ref, a_ref, b_ref, y_ref, state, *, n_heads):
    c_len = SCAN_CHUNK

    @pl.when(pl.program_id(1) == 0)
    def _():
        state[...] = jnp.zeros_like(state)

    ti = lax.broadcasted_iota(jnp.int32, (c_len, c_len), 0)
    si = lax.broadcasted_iota(jnp.int32, (c_len, c_len), 1)
    incl = si <= ti
    strict = si < ti
    tril = incl.astype(F32)
    eye = (si == ti).astype(F32)
    mid = c_len // 2 - 1

    for c in range(r_ref.shape[0] // c_len):
        rows = slice(c * c_len, (c + 1) * c_len)
        lw = lw_ref[rows, :]
        cum = jnp.dot(tril, lw, preferred_element_type=F32, precision=lax.Precision.HIGHEST)
        l_mid = cum[mid:mid + 1, :]
        l_end = cum[c_len - 1:, :]
        p_mid = jnp.exp(l_mid)
        p_end = jnp.exp(l_end)
        e_neg = jnp.exp(l_mid - cum)
        e_pos = jnp.exp(cum - l_mid)
        e_end = jnp.exp(l_end - cum)
        a_t = a_ref[rows, :] * jnp.exp(cum - lw - l_mid)
        r_t = r_ref[rows, :] * e_pos
        b_t = b_ref[rows, :] * e_neg
        k_t = k_ref[rows, :] * e_neg
        b_e = b_ref[rows, :] * e_end
        k_e = k_ref[rows, :] * e_end
        v_all = v_ref[rows, :]

        for hd in range(n_heads):
            cols = slice(hd * HEAD_DIM, (hd + 1) * HEAD_DIM)
            lhs = jnp.concatenate([a_t[:, cols], r_t[:, cols]], axis=0)
            rhs = jnp.concatenate([b_t[:, cols], k_t[:, cols]], axis=0)
            qk = _dot_nt(lhs.astype(BF16), rhs.astype(BF16))
            a_ab = jnp.where(strict, qk[:c_len, :c_len], 0.0)
            a_ak = jnp.where(strict, qk[:c_len, c_len:], 0.0)
            a_rb = jnp.where(incl, qk[c_len:, :c_len], 0.0)
            a_rk = jnp.where(incl, qk[c_len:, c_len:], 0.0)
            inv = eye + a_ab
            qb = a_ab.astype(BF16)
            q = _dot(qb, qb)
            covered = 2
            while 2 * covered < c_len:
                qb = q.astype(BF16)
                prod = _dot(qb, jnp.concatenate([qb, inv.astype(BF16)], axis=1))
                q = prod[:, :c_len]
                inv = inv + prod[:, c_len:]
                covered *= 2
            inv = inv + _dot(q.astype(BF16), inv.astype(BF16))
            v_h = v_all[:, cols].astype(BF16)
            av = _dot(jnp.concatenate([a_ak, a_rk], axis=0).astype(BF16), v_h)
            s0 = state[hd]
            sa = _dot_nt((lhs * p_mid[:, cols]).astype(BF16), s0.astype(BF16))
            u = _dot(inv.astype(BF16), (sa[:c_len] + av[:c_len]).astype(BF16))
            y = sa[c_len:] + av[c_len:] + _dot(a_rb.astype(BF16), u.astype(BF16))
            uv = jnp.concatenate([u, v_all[:, cols]], axis=0).astype(BF16)
            bk = jnp.concatenate([b_e[:, cols], k_e[:, cols]], axis=0).astype(BF16)
            state[hd] = s0 * p_end[:, cols] + _dot_tn(uv, bk)
            y_ref[rows, cols] = y


def _scan(r, lw, k, v, a, b, *, bsz, ts=256):
    n, width = r.shape
    nt = n // bsz // ts
    n_heads = width // HEAD_DIM
    spec = pl.BlockSpec((ts, width), lambda bb, t: (bb * nt + t, 0))
    return pl.pallas_call(
        functools.partial(_scan_body, n_heads=n_heads),
        grid=(bsz, nt),
        in_specs=[spec] * 6,
        out_specs=spec,
        out_shape=jax.ShapeDtypeStruct((n, width), F32),
        scratch_shapes=[pltpu.VMEM((n_heads, HEAD_DIM, HEAD_DIM), F32)],
        compiler_params=_cparams("parallel", "arbitrary"),
        name="rwkv_scan",
    )(r, lw, k, v, a, b)


def _mix_out_body(x_ref, y_ref, bonus_ref, gate_ref, yb_ref, gng_ref, gnb_ref, ones_ref, wa_ref, wb_ref, o_ref):
    ones_bd = ones_ref[...]
    y = y_ref[...]
    mu = _group_sum(y, ones_bd) * (1.0 / HEAD_DIM)
    yc = y - mu
    var = _group_sum(yc * yc, ones_bd) * (1.0 / HEAD_DIM)
    yn = yc * lax.rsqrt(var + GN_EPS) * gng_ref[...] + gnb_ref[...]
    ya = ((yn + bonus_ref[...]) * gate_ref[...]).astype(BF16)
    o_ref[...] = x_ref[...] + _dot(ya, wa_ref[...]) + _dot(yb_ref[...].astype(BF16), wb_ref[...])


def _mix_out(x, y, bonus, gate, yb, gng, gnb, ones_bd, wa, wb, *, tm=512):
    n, d = x.shape
    w = y.shape[1]
    xs = pl.BlockSpec((tm, d), lambda i: (i, 0))
    ws = pl.BlockSpec((tm, w), lambda i: (i, 0))
    consts = [gng, gnb, ones_bd, wa, wb]
    return pl.pallas_call(
        _mix_out_body,
        grid=(n // tm,),
        in_specs=[xs, ws, ws, ws, ws] + [_resident(c.shape) for c in consts],
        out_specs=xs,
        out_shape=jax.ShapeDtypeStruct((n, d), F32),
        compiler_params=_cparams("parallel"),
        name="mix_out",
    )(x, y, bonus, gate, yb, *consts)


def _row(v):
    return v.reshape(1, -1)


def kernel(x, positions, ffn1_norm, ffn1_w_gate, ffn1_w_up, ffn1_w_down, mix_norm, ffn2_norm, ffn2_w_gate, ffn2_w_up, ffn2_w_down, ab_w_in, ab_mu_shift, rwkv_w0, rwkv_w_decay_up, rwkv_a0, rwkv_w_aaa_up, rwkv_w_gate_up, rwkv_k_k, rwkv_k_a, rwkv_r_k, rwkv_gn_gain, rwkv_gn_bias, sg_ln_gain, sg_ln_bias, sg_w_spatial, sg_b_spatial, ab_w_out, attn_w_qkv, attn_b_qkv, attn_q_norm, attn_k_norm, attn_sinks, attn_w_o, attn_b_o):
    bsz, t_len, d = x.shape
    depth = ffn1_norm.shape[0]
    a_width = rwkv_w0.shape[1]
    b_width = sg_ln_gain.shape[1]
    n_groups = sg_w_spatial.shape[1]
    bf = lambda w: w.astype(BF16)

    xf = x.reshape(bsz * t_len, d)
    cos, sin = _rope_tables(positions.reshape(-1, 1))
    head_id = jnp.arange(a_width) // HEAD_DIM
    ones_bd = (head_id[:, None] == head_id[None, :]).astype(BF16)
    pairs = LANES // HEAD_DIM

    for l in range(depth):
        i = l // 2
        xf = _ffn(xf, _row(ffn1_norm[l]), bf(ffn1_w_gate[l]), bf(ffn1_w_up[l]), bf(ffn1_w_down[l]))
        if l % 2 == 0:
            lora = rwkv_w_decay_up.shape[1]
            zeros = jnp.zeros((LANES - lora, a_width), F32)
            wdec = bf(jnp.concatenate([rwkv_w_decay_up[i], zeros], axis=0))
            waaa = bf(jnp.concatenate([zeros, rwkv_w_aaa_up[i]], axis=0))
            bsp = jnp.repeat(sg_b_spatial[i].T, b_width // n_groups, axis=1)
            r, lw, k, v, a, b, gate, bonus, yb = _mix_in(
                xf, _row(mix_norm[l]), bf(ab_w_in[i]), _row(ab_mu_shift[i]), _row(rwkv_w0[i]), wdec,
                _row(rwkv_a0[i]), waaa, bf(rwkv_w_gate_up[i]), _row(rwkv_k_k[i]), _row(rwkv_k_a[i]),
                _row(rwkv_r_k[i]), _row(sg_ln_gain[i]), _row(sg_ln_bias[i]), sg_w_spatial[i], bsp, ones_bd,
                bsz=bsz)
            y = _scan(r, lw, k, v, a, b, bsz=bsz)
            xf = _mix_out(xf, y, bonus, gate, yb, _row(rwkv_gn_gain[i]), _row(rwkv_gn_bias[i]), ones_bd,
                          bf(ab_w_out[i][:a_width]), bf(ab_w_out[i][a_width:]))
        else:
            xf = _attn(xf, cos, sin, _row(mix_norm[l]), bf(attn_w_qkv[i]), _row(attn_b_qkv[i]),
                       _row(jnp.tile(attn_q_norm[i], pairs)), _row(jnp.tile(attn_k_norm[i], pairs)),
                       attn_sinks[i], bf(attn_w_o[i]), _row(attn_b_o[i]), bsz=bsz)
        xf = _ffn(xf, _row(ffn2_norm[l]), bf(ffn2_w_gate[l]), bf(ffn2_w_up[l]), bf(ffn2_w_down[l]))
    return xf.reshape(bsz, t_len, d)
```

```python
import functools
import math

import jax
import jax.numpy as jnp
from jax import lax
from jax.experimental import pallas as pl
from jax.experimental.pallas import tpu as pltpu

F32 = jnp.float32
BF16 = jnp.bfloat16

HEAD_DIM = 64
RMS_EPS = 1e-6
GN_EPS = 64e-5
LN_EPS = 1e-5
ROPE_THETA = 10000.0
WINDOW = 128
SCAN_CHUNK = 64
LANES = 128
VMEM_LIMIT = 56 * 1024 * 1024


def _cparams(*sem):
    return pltpu.CompilerParams(dimension_semantics=sem, vmem_limit_bytes=VMEM_LIMIT)


def _resident(shape):
    nd = len(shape)
    return pl.BlockSpec(shape, lambda *_: (0,) * nd, pipeline_mode=pl.Buffered(1))


def _dot(a, b):
    return jnp.dot(a, b, preferred_element_type=F32)


def _dot_nt(a, b):
    return lax.dot_general(a, b, (((1,), (1,)), ((), ())), preferred_element_type=F32)


def _dot_tn(a, b):
    return lax.dot_general(a, b, (((0,), (0,)), ((), ())), preferred_element_type=F32)


def _rms(x, gain):
    return x * lax.rsqrt(jnp.mean(x * x, axis=-1, keepdims=True) + RMS_EPS) * gain


def _split_bf16(x):
    hi = x.astype(BF16)
    lo = (x - hi.astype(F32)).astype(BF16)
    return hi, lo


def _group_sum(x, ones_bd):
    hi, lo = _split_bf16(x)
    return _dot(hi, ones_bd) + _dot(lo, ones_bd)


def _ffn_body(x_ref, gain_ref, wg_ref, wu_ref, wd_ref, o_ref, *, n_chunks):
    x = x_ref[...]
    h = _rms(x, gain_ref[...]).astype(BF16)
    fc = wg_ref.shape[1] // n_chunks
    acc = jnp.zeros_like(x)
    for c in range(n_chunks):
        g = _dot(h, wg_ref[:, c * fc:(c + 1) * fc])
        u = _dot(h, wu_ref[:, c * fc:(c + 1) * fc])
        act = (g * jax.nn.sigmoid(g) * u).astype(BF16)
        acc = acc + _dot(act, wd_ref[c * fc:(c + 1) * fc, :])
    o_ref[...] = x + 0.5 * acc


def _ffn(x, gain, wg, wu, wd, *, tm=512, n_chunks=2):
    n, d = x.shape
    f = wg.shape[1]
    return pl.pallas_call(
        functools.partial(_ffn_body, n_chunks=n_chunks),
        grid=(n // tm,),
        in_specs=[pl.BlockSpec((tm, d), lambda i: (i, 0)),
                  _resident((1, d)), _resident((d, f)), _resident((d, f)), _resident((f, d))],
        out_specs=pl.BlockSpec((tm, d), lambda i: (i, 0)),
        out_shape=jax.ShapeDtypeStruct((n, d), F32),
        compiler_params=_cparams("parallel"),
        name="ffn",
    )(x, gain, wg, wu, wd)


def _rope_body(pos_ref, freq_ref, cos_ref, sin_ref):
    ang = pos_ref[...].astype(F32) * freq_ref[...]
    lane = lax.broadcasted_iota(jnp.int32, ang.shape, 1)
    first_half = (lane % HEAD_DIM) < HEAD_DIM // 2
    cos_ref[...] = jnp.cos(ang)
    s = jnp.sin(ang)
    sin_ref[...] = jnp.where(first_half, -s, s)


def _rope_tables(pos, *, tm=2048):
    n = pos.shape[0]
    tm = min(tm, n)
    half = HEAD_DIM // 2
    inv_freq = ROPE_THETA ** (-jnp.arange(0, HEAD_DIM, 2, dtype=F32) / HEAD_DIM)
    freq = jnp.tile(inv_freq, LANES // half)[None, :]
    return pl.pallas_call(
        _rope_body,
        grid=(n // tm,),
        in_specs=[pl.BlockSpec((tm, 1), lambda i: (i, 0)), _resident((1, LANES))],
        out_specs=[pl.BlockSpec((tm, LANES), lambda i: (i, 0))] * 2,
        out_shape=[jax.ShapeDtypeStruct((n, LANES), F32)] * 2,
        compiler_params=_cparams("parallel"),
        name="rope_tables",
    )(pos, freq)


def _attn_body(sink_ref, x_ref, cos_ref, sin_ref, gain_ref, wqkv_ref, bqkv_ref, qg_ref, kg_ref, wo_ref, bo_ref,
               o_ref, k2_prev, v2_prev, *, n_q, n_kv):
    t = pl.program_id(1)
    tq = x_ref.shape[0]
    nblk = tq // WINDOW
    qw = n_q * HEAD_DIM
    kw = n_kv * HEAD_DIM
    per_kv = n_q // n_kv

    @pl.when(t == 0)
    def _():
        k2_prev[...] = jnp.zeros_like(k2_prev)
        v2_prev[...] = jnp.zeros_like(v2_prev)

    x = x_ref[...]
    h = _rms(x, gain_ref[...]).astype(BF16)
    qkv = _dot(h, wqkv_ref[...]) + bqkv_ref[...]

    lane = lax.broadcasted_iota(jnp.int32, (tq, LANES), 1)
    lo = lane < HEAD_DIM
    first_half = (lane % HEAD_DIM) < HEAD_DIM // 2
    cos = cos_ref[...]
    sin = sin_ref[...]

    def norm_rope(z, gain_pair, extra):
        zz = z * z
        s_lo = jnp.sum(jnp.where(lo, zz, 0.0), axis=-1, keepdims=True)
        s_all = jnp.sum(zz, axis=-1, keepdims=True)
        inv_lo = lax.rsqrt(s_lo * (1.0 / HEAD_DIM) + RMS_EPS)
        inv_hi = lax.rsqrt((s_all - s_lo) * (1.0 / HEAD_DIM) + RMS_EPS)
        zg = z * gain_pair
        partner = jnp.where(first_half, pltpu.roll(zg, LANES - HEAD_DIM // 2, axis=1),
                            pltpu.roll(zg, HEAD_DIM // 2, axis=1))
        rot = zg * cos + partner * sin
        return rot * (jnp.where(lo, inv_lo, inv_hi) * extra)

    scale = HEAD_DIM ** -0.5
    q_pairs = [norm_rope(qkv[:, p * LANES:(p + 1) * LANES], qg_ref[...], scale).astype(BF16)
               for p in range(qw // LANES)]
    k_pairs = [norm_rope(qkv[:, qw + p * LANES: qw + (p + 1) * LANES], kg_ref[...], 1.0)
               for p in range(kw // LANES)]
    v_pairs = [qkv[:, qw + kw + p * LANES: qw + kw + (p + 1) * LANES] for p in range(kw // LANES)]

    k2, v_top, v_bot = [], [], []
    for g in range(n_kv):
        kp = k_pairs[g // 2]
        vp = v_pairs[g // 2]
        kr = pltpu.roll(kp, HEAD_DIM, axis=1)
        vr = pltpu.roll(vp, HEAD_DIM, axis=1)
        if g % 2 == 0:
            k2.append(jnp.where(lo, kp, kr).astype(BF16))
            v_top.append(jnp.where(lo, vp, 0.0).astype(BF16))
            v_bot.append(jnp.where(lo, 0.0, vr).astype(BF16))
        else:
            k2.append(jnp.where(lo, kr, kp).astype(BF16))
            v_top.append(jnp.where(lo, vr, 0.0).astype(BF16))
            v_bot.append(jnp.where(lo, 0.0, vp).astype(BF16))

    qi = lax.broadcasted_iota(jnp.int32, (WINDOW, 2 * WINDOW), 0)
    kj = lax.broadcasted_iota(jnp.int32, (WINDOW, 2 * WINDOW), 1)
    dist = qi + WINDOW - kj
    in_window = (dist >= 0) & (dist < WINDOW)
    lo_blk = lax.broadcasted_iota(jnp.int32, (WINDOW, LANES), 1) < HEAD_DIM

    o_rows = []
    for blk in range(nblk):
        rows = slice(blk * WINDOW, (blk + 1) * WINDOW)
        if blk == 0:
            valid = in_window & (kj >= jnp.where(t > 0, 0, WINDOW))
        else:
            valid = in_window
        valid2 = jnp.concatenate([valid, valid], axis=0)
        o_pairs = []
        for g in range(n_kv):
            if blk == 0:
                kprev, vtp, vbp = k2_prev[g], v2_prev[g, 0], v2_prev[g, 1]
            else:
                prows = slice((blk - 1) * WINDOW, blk * WINDOW)
                kprev, vtp, vbp = k2[g][prows], v_top[g][prows], v_bot[g][prows]
            kband = jnp.concatenate([kprev, k2[g][rows]], axis=0)
            vband = jnp.concatenate([vtp, v_top[g][rows], vbp, v_bot[g][rows]], axis=0)
            for pp in range(per_kv // 2):
                pair = g * (per_kv // 2) + pp
                qp = q_pairs[pair][rows]
                zero = jnp.zeros_like(qp)
                lhs = jnp.concatenate([jnp.where(lo_blk, qp, zero), jnp.where(lo_blk, zero, qp)], axis=0)
                s = _dot_nt(lhs, kband)
                s = jnp.where(valid2, s, -jnp.inf)
                row2 = lax.broadcasted_iota(jnp.int32, (2 * WINDOW, 1), 0)
                sink = jnp.where(row2 < WINDOW, sink_ref[2 * pair], sink_ref[2 * pair + 1])
                m = jnp.maximum(jnp.max(s, axis=-1, keepdims=True), sink)
                p = jnp.exp(s - m)
                denom = jnp.sum(p, axis=-1, keepdims=True) + jnp.exp(sink - m)
                pb = p.astype(BF16)
                p_cat = jnp.concatenate([pb[:WINDOW], pb[WINDOW:]], axis=1)
                o = _dot(p_cat, vband)
                inv = 1.0 / denom
                o_pairs.append(o * jnp.where(lo_blk, inv[:WINDOW], inv[WINDOW:]))
        o_rows.append(jnp.concatenate(o_pairs, axis=1).astype(BF16))
    o_all = jnp.concatenate(o_rows, axis=0)

    last = slice((nblk - 1) * WINDOW, nblk * WINDOW)
    for g in range(n_kv):
        k2_prev[g] = k2[g][last]
        v2_prev[g, 0] = v_top[g][last]
        v2_prev[g, 1] = v_bot[g][last]

    o_ref[...] = x + _dot(o_all, wo_ref[...]) + bo_ref[...]


def _attn(x, cos, sin, gain, wqkv, bqkv, qg, kg, sinks, wo, bo, *, bsz, tq=256):
    n, d = x.shape
    t_len = n // bsz
    nt = t_len // tq
    n_q = wo.shape[0] // HEAD_DIM
    n_kv = (wqkv.shape[1] - wo.shape[0]) // (2 * HEAD_DIM)
    row = lambda b, t, *_: (b * nt + t, 0)
    grid_spec = pltpu.PrefetchScalarGridSpec(
        num_scalar_prefetch=1,
        grid=(bsz, nt),
        in_specs=[pl.BlockSpec((tq, d), row), pl.BlockSpec((tq, LANES), row), pl.BlockSpec((tq, LANES), row),
                  _resident((1, d)), _resident(wqkv.shape), _resident(bqkv.shape),
                  _resident((1, LANES)), _resident((1, LANES)), _resident(wo.shape), _resident(bo.shape)],
        out_specs=pl.BlockSpec((tq, d), row),
        scratch_shapes=[pltpu.VMEM((n_kv, WINDOW, LANES), BF16), pltpu.VMEM((n_kv, 2, WINDOW, LANES), BF16)],
    )
    return pl.pallas_call(
        functools.partial(_attn_body, n_q=n_q, n_kv=n_kv),
        grid_spec=grid_spec,
        out_shape=jax.ShapeDtypeStruct((n, d), F32),
        compiler_params=_cparams("parallel", "arbitrary"),
        name="swa_attention",
    )(sinks, x, cos, sin, gain, wqkv, bqkv, qg, kg, wo, bo)


def _mix_in_body(x_ref, xp_ref, gain_ref, win_ref, mu_ref, w0_ref, wdec_ref, a0_ref, waaa_ref, wgate_ref,
                 kk_ref, ka_ref, rk_ref, lng_ref, lnb_ref, wsp_ref, bsp_ref, ones_ref,
                 r_o, lw_o, k_o, v_o, a_o, b_o, gate_o, bonus_o, yb_o, *, a_width, b_width):
    t = pl.program_id(1)
    tm = x_ref.shape[0]
    a_cols = mu_ref.shape[1]
    gain = gain_ref[...]
    h = _rms(x_ref[...], gain).astype(BF16)
    hp = _rms(xp_ref[...], gain).astype(BF16)
    z = _dot(h, win_ref[...])
    zp = _dot(hp, win_ref[:, :a_cols])
    za = z[:, :a_cols]
    prev_row = jnp.where(t > 0, zp[xp_ref.shape[0] - 1:, :], 0.0)
    row = lax.broadcasted_iota(jnp.int32, (tm, 1), 0)
    shifted = jnp.where(row == 0, prev_row, pltpu.roll(za, 1, axis=0))
    za = za + (shifted - za) * mu_ref[...]

    aw = a_width
    r = za[:, :aw]
    k = za[:, aw:2 * aw]
    v = za[:, 2 * aw:3 * aw]
    xwa = za[:, 3 * aw:3 * aw + LANES]
    xg = za[:, 3 * aw + LANES:]
    y_dec = w0_ref[...] + _dot(jnp.tanh(xwa).astype(BF16), wdec_ref[...])
    lw = (-math.exp(-0.5)) * jax.nn.sigmoid(y_dec)
    iclr = jax.nn.sigmoid(a0_ref[...] + _dot(xwa.astype(BF16), waaa_ref[...]))
    gate = _dot(jax.nn.sigmoid(xg).astype(BF16), wgate_ref[...])
    ones_bd = ones_ref[...]
    kk = k * kk_ref[...]
    kk = kk / jnp.maximum(jnp.sqrt(_group_sum(kk * kk, ones_bd)), 1e-12)
    k = k * (1.0 + (iclr - 1.0) * ka_ref[...])
    r_o[...] = r
    lw_o[...] = lw
    k_o[...] = k
    v_o[...] = v
    a_o[...] = -kk
    b_o[...] = kk * iclr
    gate_o[...] = gate
    bonus_o[...] = _group_sum(r * k * rk_ref[...], ones_bd) * v

    zb = z[:, a_cols:]
    zb = 0.5 * zb * (1.0 + lax.erf(zb * math.sqrt(0.5)))
    u = zb[:, :b_width]
    s = zb[:, b_width:]
    mean = jnp.mean(s, axis=-1, keepdims=True)
    sc = s - mean
    var = jnp.mean(sc * sc, axis=-1, keepdims=True)
    s = (sc * lax.rsqrt(var + LN_EPS) * lng_ref[...] + lnb_ref[...]).astype(BF16)
    ti = lax.broadcasted_iota(jnp.int32, (WINDOW, WINDOW), 0)
    si = lax.broadcasted_iota(jnp.int32, (WINDOW, WINDOW), 1)
    n_groups = wsp_ref.shape[0]
    gd = b_width // n_groups
    for g in range(n_groups):
        w_g = jnp.where(si <= ti, wsp_ref[g], 0.0).astype(BF16)
        for c in range(tm // WINDOW):
            rows = slice(c * WINDOW, (c + 1) * WINDOW)
            cols = slice(g * gd, (g + 1) * gd)
            mixed = _dot(w_g, s[rows, cols]) + bsp_ref[:, cols]
            yb_o[rows, cols] = u[rows, cols] * mixed


def _mix_in(x, gain, win, mu, w0, wdec, a0, waaa, wgate, kk, ka, rk, lng, lnb, wsp, bsp, ones_bd, *, bsz, tm=256):
    n, d = x.shape
    t_len = n // bsz
    nt = t_len // tm
    a_width = w0.shape[1]
    b_width = lng.shape[1]
    prev_rows = 8
    row = lambda b, t: (b * nt + t, 0)
    prev = lambda b, t: (jnp.maximum((b * nt + t) * (tm // prev_rows) - 1, 0), 0)
    consts = [gain, win, mu, w0, wdec, a0, waaa, wgate, kk, ka, rk, lng, lnb, wsp, bsp, ones_bd]
    out_spec = pl.BlockSpec((tm, a_width), row)
    return pl.pallas_call(
        functools.partial(_mix_in_body, a_width=a_width, b_width=b_width),
        grid=(bsz, nt),
        in_specs=[pl.BlockSpec((tm, d), row), pl.BlockSpec((prev_rows, d), prev)] + [_resident(c.shape) for c in consts],
        out_specs=[out_spec] * 9,
        out_shape=[jax.ShapeDtypeStruct((n, a_width), F32)] * 9,
        compiler_params=_cparams("parallel", "parallel"),
        name="mix_in",
    )(x, x, *consts)


def _scan_body(r_ref, lw_ref, k_ref, v_ref, a_ref, b_ref, y_ref, state, *, n_heads):
    c_len = SCAN_CHUNK

    @pl.when(pl.program_id(1) == 0)
    def _():
        state[...] = jnp.zeros_like(state)

    ti = lax.broadcasted_iota(jnp.int32, (c_len, c_len), 0)
    si = lax.broadcasted_iota(jnp.int32, (c_len, c_len), 1)
    incl = si <= ti
    strict = si < ti
    tril = incl.astype(F32)
    eye = (si == ti).astype(F32)
    mid = c_len // 2 - 1

    n_chunks = r_ref.shape[0] // c_len
    heads = range(n_heads)
    hcols = [slice(hd * HEAD_DIM, (hd + 1) * HEAD_DIM) for hd in heads]

    pre = []
    for c in range(n_chunks):
        rows = slice(c * c_len, (c + 1) * c_len)
        lw = lw_ref[rows, :]
        cum = jnp.dot(tril, lw, preferred_element_type=F32, precision=lax.Precision.HIGHEST)
        l_mid = cum[mid:mid + 1, :]
        l_end = cum[c_len - 1:, :]
        e_neg = jnp.exp(l_mid - cum)
        e_pos = jnp.exp(cum - l_mid)
        e_end = jnp.exp(l_end - cum)
        pre.append(dict(
            rows=rows,
            p_mid=jnp.exp(l_mid), p_end=jnp.exp(l_end),
            a_t=a_ref[rows, :] * jnp.exp(cum - lw - l_mid),
            r_t=r_ref[rows, :] * e_pos,
            b_t=b_ref[rows, :] * e_neg,
            k_t=k_ref[rows, :] * e_neg,
            b_e=b_ref[rows, :] * e_end,
            k_e=k_ref[rows, :] * e_end,
            v=v_ref[rows, :]))

    inst = [(c, hd) for c in range(n_chunks) for hd in heads]
    lhs = {(c, hd): jnp.concatenate([pre[c]["a_t"][:, hcols[hd]], pre[c]["r_t"][:, hcols[hd]]], axis=0)
           for c, hd in inst}
    qk = {}
    for c, hd in inst:
        rhs = jnp.concatenate([pre[c]["b_t"][:, hcols[hd]], pre[c]["k_t"][:, hcols[hd]]], axis=0)
        qk[c, hd] = _dot_nt(lhs[c, hd].astype(BF16), rhs.astype(BF16))
    a_ab = {i: jnp.where(strict, qk[i][:c_len, :c_len], 0.0) for i in inst}
    a_rb = {i: jnp.where(incl, qk[i][c_len:, :c_len], 0.0).astype(BF16) for i in inst}
    a_xk = {i: jnp.concatenate([jnp.where(strict, qk[i][:c_len, c_len:], 0.0),
                                jnp.where(incl, qk[i][c_len:, c_len:], 0.0)], axis=0).astype(BF16) for i in inst}
    v_h = {(c, hd): pre[c]["v"][:, hcols[hd]].astype(BF16) for c, hd in inst}
    av = {i: _dot(a_xk[i], v_h[i]) for i in inst}
    inv = {i: eye + a_ab[i] for i in inst}
    q = {}
    for i in inst:
        qb = a_ab[i].astype(BF16)
        q[i] = _dot(qb, qb)
    covered = 2
    while 2 * covered < c_len:
        for i in inst:
            qb = q[i].astype(BF16)
            prod = _dot(qb, jnp.concatenate([qb, inv[i].astype(BF16)], axis=1))
            q[i] = prod[:, :c_len]
            inv[i] = inv[i] + prod[:, c_len:]
        covered *= 2
    for i in inst:
        inv[i] = (inv[i] + _dot(q[i].astype(BF16), inv[i].astype(BF16))).astype(BF16)
    bk = {(c, hd): jnp.concatenate([pre[c]["b_e"][:, hcols[hd]], pre[c]["k_e"][:, hcols[hd]]], axis=0).astype(BF16)
          for c, hd in inst}
    lhs_s = {(c, hd): (lhs[c, hd] * pre[c]["p_mid"][:, hcols[hd]]).astype(BF16) for c, hd in inst}

    s_cur = [state[hd] for hd in heads]
    for c in range(n_chunks):
        sa = [_dot_nt(lhs_s[c, hd], s_cur[hd].astype(BF16)) for hd in heads]
        u = [_dot(inv[c, hd], (sa[hd][:c_len] + av[c, hd][:c_len]).astype(BF16)) for hd in heads]
        for hd in heads:
            uv = jnp.concatenate([u[hd].astype(BF16), v_h[c, hd]], axis=0)
            s_cur[hd] = s_cur[hd] * pre[c]["p_end"][:, hcols[hd]] + _dot_tn(uv, bk[c, hd])
        for hd in heads:
            y = sa[hd][c_len:] + av[c, hd][c_len:] + _dot(a_rb[c, hd], u[hd].astype(BF16))
            y_ref[pre[c]["rows"], hcols[hd]] = y
    for hd in heads:
        state[hd] = s_cur[hd]


def _scan(r, lw, k, v, a, b, *, bsz, ts=256):
    n, width = r.shape
    nt = n // bsz // ts
    n_heads = width // HEAD_DIM
    spec = pl.BlockSpec((ts, width), lambda bb, t: (bb * nt + t, 0))
    return pl.pallas_call(
        functools.partial(_scan_body, n_heads=n_heads),
        grid=(bsz, nt),
        in_specs=[spec] * 6,
        out_specs=spec,
        out_shape=jax.ShapeDtypeStruct((n, width), F32),
        scratch_shapes=[pltpu.VMEM((n_heads, HEAD_DIM, HEAD_DIM), F32)],
        compiler_params=_cparams("parallel", "arbitrary"),
        name="rwkv_scan",
    )(r, lw, k, v, a, b)


def _mix_out_body(x_ref, y_ref, bonus_ref, gate_ref, yb_ref, gng_ref, gnb_ref, ones_ref, wa_ref, wb_ref, o_ref):
    ones_bd = ones_ref[...]
    y = y_ref[...]
    mu = _group_sum(y, ones_bd) * (1.0 / HEAD_DIM)
    yc = y - mu
    var = _group_sum(yc * yc, ones_bd) * (1.0 / HEAD_DIM)
    yn = yc * lax.rsqrt(var + GN_EPS) * gng_ref[...] + gnb_ref[...]
    ya = ((yn + bonus_ref[...]) * gate_ref[...]).astype(BF16)
    o_ref[...] = x_ref[...] + _dot(ya, wa_ref[...]) + _dot(yb_ref[...].astype(BF16), wb_ref[...])


def _mix_out(x, y, bonus, gate, yb, gng, gnb, ones_bd, wa, wb, *, tm=512):
    n, d = x.shape
    w = y.shape[1]
    xs = pl.BlockSpec((tm, d), lambda i: (i, 0))
    ws = pl.BlockSpec((tm, w), lambda i: (i, 0))
    consts = [gng, gnb, ones_bd, wa, wb]
    return pl.pallas_call(
        _mix_out_body,
        grid=(n // tm,),
        in_specs=[xs, ws, ws, ws, ws] + [_resident(c.shape) for c in consts],
        out_specs=xs,
        out_shape=jax.ShapeDtypeStruct((n, d), F32),
        compiler_params=_cparams("parallel"),
        name="mix_out",
    )(x, y, bonus, gate, yb, *consts)


def _row(v):
    return v.reshape(1, -1)


def kernel(x, positions, ffn1_norm, ffn1_w_gate, ffn1_w_up, ffn1_w_down, mix_norm, ffn2_norm, ffn2_w_gate, ffn2_w_up, ffn2_w_down, ab_w_in, ab_mu_shift, rwkv_w0, rwkv_w_decay_up, rwkv_a0, rwkv_w_aaa_up, rwkv_w_gate_up, rwkv_k_k, rwkv_k_a, rwkv_r_k, rwkv_gn_gain, rwkv_gn_bias, sg_ln_gain, sg_ln_bias, sg_w_spatial, sg_b_spatial, ab_w_out, attn_w_qkv, attn_b_qkv, attn_q_norm, attn_k_norm, attn_sinks, attn_w_o, attn_b_o):
    bsz, t_len, d = x.shape
    depth = ffn1_norm.shape[0]
    a_width = rwkv_w0.shape[1]
    b_width = sg_ln_gain.shape[1]
    n_groups = sg_w_spatial.shape[1]
    bf = lambda w: w.astype(BF16)

    xf = x.reshape(bsz * t_len, d)
    cos, sin = _rope_tables(positions.reshape(-1, 1))
    head_id = jnp.arange(a_width) // HEAD_DIM
    ones_bd = (head_id[:, None] == head_id[None, :]).astype(BF16)
    pairs = LANES // HEAD_DIM

    for l in range(depth):
        i = l // 2
        xf = _ffn(xf, _row(ffn1_norm[l]), bf(ffn1_w_gate[l]), bf(ffn1_w_up[l]), bf(ffn1_w_down[l]))
        if l % 2 == 0:
            lora = rwkv_w_decay_up.shape[1]
            zeros = jnp.zeros((LANES - lora, a_width), F32)
            wdec = bf(jnp.concatenate([rwkv_w_decay_up[i], zeros], axis=0))
            waaa = bf(jnp.concatenate([zeros, rwkv_w_aaa_up[i]], axis=0))
            bsp = jnp.repeat(sg_b_spatial[i].T, b_width // n_groups, axis=1)
            r, lw, k, v, a, b, gate, bonus, yb = _mix_in(
                xf, _row(mix_norm[l]), bf(ab_w_in[i]), _row(ab_mu_shift[i]), _row(rwkv_w0[i]), wdec,
                _row(rwkv_a0[i]), waaa, bf(rwkv_w_gate_up[i]), _row(rwkv_k_k[i]), _row(rwkv_k_a[i]),
                _row(rwkv_r_k[i]), _row(sg_ln_gain[i]), _row(sg_ln_bias[i]), sg_w_spatial[i], bsp, ones_bd,
                bsz=bsz)
            y = _scan(r, lw, k, v, a, b, bsz=bsz)
            xf = _mix_out(xf, y, bonus, gate, yb, _row(rwkv_gn_gain[i]), _row(rwkv_gn_bias[i]), ones_bd,
                          bf(ab_w_out[i][:a_width]), bf(ab_w_out[i][a_width:]))
        else:
            xf = _attn(xf, cos, sin, _row(mix_norm[l]), bf(attn_w_qkv[i]), _row(attn_b_qkv[i]),
                       _row(jnp.tile(attn_q_norm[i], pairs)), _row(jnp.tile(attn_k_norm[i], pairs)),
                       attn_sinks[i], bf(attn_w_o[i]), _row(attn_b_o[i]), bsz=bsz)
        xf = _ffn(xf, _row(ffn2_norm[l]), bf(ffn2_w_gate[l]), bf(ffn2_w_up[l]), bf(ffn2_w_down[l]))
    return xf.reshape(bsz, t_len, d)
```

```python
import functools
import math

import jax
import jax.numpy as jnp
from jax import lax
from jax.experimental import pallas as pl
from jax.experimental.pallas import tpu as pltpu

F32 = jnp.float32
BF16 = jnp.bfloat16

HEAD_DIM = 64
RMS_EPS = 1e-6
GN_EPS = 64e-5
LN_EPS = 1e-5
ROPE_THETA = 10000.0
WINDOW = 128
SCAN_CHUNK = 64
LANES = 128
VMEM_LIMIT = 56 * 1024 * 1024


def _cparams(*sem):
    return pltpu.CompilerParams(dimension_semantics=sem, vmem_limit_bytes=VMEM_LIMIT)


def _resident(shape):
    nd = len(shape)
    return pl.BlockSpec(shape, lambda *_: (0,) * nd, pipeline_mode=pl.Buffered(1))


def _dot(a, b):
    return jnp.dot(a, b, preferred_element_type=F32)


def _dot_nt(a, b):
    return lax.dot_general(a, b, (((1,), (1,)), ((), ())), preferred_element_type=F32)


def _dot_tn(a, b):
    return lax.dot_general(a, b, (((0,), (0,)), ((), ())), preferred_element_type=F32)


def _rms(x, gain):
    return x * lax.rsqrt(jnp.mean(x * x, axis=-1, keepdims=True) + RMS_EPS) * gain


def _split_bf16(x):
    hi = x.astype(BF16)
    lo = (x - hi.astype(F32)).astype(BF16)
    return hi, lo


def _group_sum(x, ones_bd):
    hi, lo = _split_bf16(x)
    return _dot(hi, ones_bd) + _dot(lo, ones_bd)


def _ffn_body(x_ref, gain_ref, wg_ref, wu_ref, wd_ref, o_ref, *, n_chunks):
    x = x_ref[...]
    h = _rms(x, gain_ref[...]).astype(BF16)
    fc = wg_ref.shape[1] // n_chunks
    acc = jnp.zeros_like(x)
    for c in range(n_chunks):
        g = _dot(h, wg_ref[:, c * fc:(c + 1) * fc])
        u = _dot(h, wu_ref[:, c * fc:(c + 1) * fc])
        act = (g * jax.nn.sigmoid(g) * u).astype(BF16)
        acc = acc + _dot(act, wd_ref[c * fc:(c + 1) * fc, :])
    o_ref[...] = x + 0.5 * acc


def _ffn(x, gain, wg, wu, wd, *, tm=1024, n_chunks=11):
    n, d = x.shape
    tm = min(tm, n)
    f = wg.shape[1]
    return pl.pallas_call(
        functools.partial(_ffn_body, n_chunks=n_chunks),
        grid=(n // tm,),
        in_specs=[pl.BlockSpec((tm, d), lambda i: (i, 0)),
                  _resident((1, d)), _resident((d, f)), _resident((d, f)), _resident((f, d))],
        out_specs=pl.BlockSpec((tm, d), lambda i: (i, 0)),
        out_shape=jax.ShapeDtypeStruct((n, d), F32),
        compiler_params=_cparams("parallel"),
        name="ffn",
    )(x, gain, wg, wu, wd)


def _rope_body(pos_ref, freq_ref, cos_ref, sin_ref):
    ang = pos_ref[...].astype(F32) * freq_ref[...]
    lane = lax.broadcasted_iota(jnp.int32, ang.shape, 1)
    first_half = (lane % HEAD_DIM) < HEAD_DIM // 2
    cos_ref[...] = jnp.cos(ang)
    s = jnp.sin(ang)
    sin_ref[...] = jnp.where(first_half, -s, s)


def _rope_tables(pos, *, tm=2048):
    n = pos.shape[0]
    tm = min(tm, n)
    half = HEAD_DIM // 2
    inv_freq = ROPE_THETA ** (-jnp.arange(0, HEAD_DIM, 2, dtype=F32) / HEAD_DIM)
    freq = jnp.tile(inv_freq, LANES // half)[None, :]
    return pl.pallas_call(
        _rope_body,
        grid=(n // tm,),
        in_specs=[pl.BlockSpec((tm, 1), lambda i: (i, 0)), _resident((1, LANES))],
        out_specs=[pl.BlockSpec((tm, LANES), lambda i: (i, 0))] * 2,
        out_shape=[jax.ShapeDtypeStruct((n, LANES), F32)] * 2,
        compiler_params=_cparams("parallel"),
        name="rope_tables",
    )(pos, freq)


def _attn_body(sink_ref, x_ref, cos_ref, sin_ref, gain_ref, wqkv_ref, bqkv_ref, qg_ref, kg_ref, wo_ref, bo_ref,
               ones_ref, o_ref, k2_prev, v2_prev, *, n_q, n_kv, wave):
    t = pl.program_id(1)
    tq = x_ref.shape[0]
    nblk = tq // WINDOW
    qw = n_q * HEAD_DIM
    kw = n_kv * HEAD_DIM
    per_kv = n_q // n_kv

    @pl.when(t == 0)
    def _():
        k2_prev[...] = jnp.zeros_like(k2_prev)
        v2_prev[...] = jnp.zeros_like(v2_prev)

    x = x_ref[...]
    h = _rms(x, gain_ref[...]).astype(BF16)
    qkv = _dot(h, wqkv_ref[...]) + bqkv_ref[...]

    lane = lax.broadcasted_iota(jnp.int32, (tq, LANES), 1)
    lo = lane < HEAD_DIM
    first_half = (lane % HEAD_DIM) < HEAD_DIM // 2
    cos = cos_ref[...]
    sin = sin_ref[...]

    ones_bd = ones_ref[...]
    bd = ones_bd.shape[0]

    def norm_rope(z, gain_pair, extra):
        zz_hi, zz_lo = _split_bf16(z * z)
        out = []
        for c in range(z.shape[1] // bd):
            cols = slice(c * bd, (c + 1) * bd)
            ssq = _dot(zz_hi[:, cols], ones_bd) + _dot(zz_lo[:, cols], ones_bd)
            inv = lax.rsqrt(ssq * (1.0 / HEAD_DIM) + RMS_EPS) * extra
            for p in range(bd // LANES):
                pc = slice(p * LANES, (p + 1) * LANES)
                zg = z[:, c * bd + p * LANES: c * bd + (p + 1) * LANES] * gain_pair
                partner = jnp.where(first_half, pltpu.roll(zg, LANES - HEAD_DIM // 2, axis=1),
                                    pltpu.roll(zg, HEAD_DIM // 2, axis=1))
                out.append((zg * cos + partner * sin) * inv[:, pc])
        return out

    scale = HEAD_DIM ** -0.5
    q_pairs = [qp.astype(BF16) for qp in norm_rope(qkv[:, :qw], qg_ref[...], scale)]
    k_pairs = norm_rope(qkv[:, qw:qw + kw], kg_ref[...], 1.0)
    v_pairs = [qkv[:, qw + kw + p * LANES: qw + kw + (p + 1) * LANES] for p in range(kw // LANES)]

    k2, v_top, v_bot = [], [], []
    for g in range(n_kv):
        kp = k_pairs[g // 2]
        vp = v_pairs[g // 2]
        kr = pltpu.roll(kp, HEAD_DIM, axis=1)
        vr = pltpu.roll(vp, HEAD_DIM, axis=1)
        if g % 2 == 0:
            k2.append(jnp.where(lo, kp, kr).astype(BF16))
            v_top.append(jnp.where(lo, vp, 0.0).astype(BF16))
            v_bot.append(jnp.where(lo, 0.0, vr).astype(BF16))
        else:
            k2.append(jnp.where(lo, kr, kp).astype(BF16))
            v_top.append(jnp.where(lo, vr, 0.0).astype(BF16))
            v_bot.append(jnp.where(lo, 0.0, vp).astype(BF16))

    qi = lax.broadcasted_iota(jnp.int32, (WINDOW, 2 * WINDOW), 0)
    kj = lax.broadcasted_iota(jnp.int32, (WINDOW, 2 * WINDOW), 1)
    dist = qi + WINDOW - kj
    in_window = (dist >= 0) & (dist < WINDOW)
    lo_blk = lax.broadcasted_iota(jnp.int32, (WINDOW, LANES), 1) < HEAD_DIM
    oc_row = lax.broadcasted_iota(jnp.int32, (4 * WINDOW, LANES), 0)
    oc_lane = lax.broadcasted_iota(jnp.int32, (4 * WINDOW, LANES), 1)
    ones_cols = ((oc_row < 2 * WINDOW) == (oc_lane < HEAD_DIM)).astype(BF16)

    n_pairs = per_kv // 2
    inst = [(blk, g, pp) for blk in range(nblk) for g in range(n_kv) for pp in range(n_pairs)]
    rows_of = lambda blk: slice(blk * WINDOW, (blk + 1) * WINDOW)
    row2 = lax.broadcasted_iota(jnp.int32, (2 * WINDOW, 1), 0)
    valid_first = in_window & (kj >= jnp.where(t > 0, 0, WINDOW))
    valid2 = {0: jnp.concatenate([valid_first, valid_first], axis=0)}
    if nblk > 1:
        valid2[1] = jnp.concatenate([in_window, in_window], axis=0)
    kband, vband = {}, {}
    for blk in range(nblk):
        for g in range(n_kv):
            if blk == 0:
                kprev, vtp, vbp = k2_prev[g], v2_prev[g, 0], v2_prev[g, 1]
            else:
                prows = rows_of(blk - 1)
                kprev, vtp, vbp = k2[g][prows], v_top[g][prows], v_bot[g][prows]
            rows = rows_of(blk)
            kband[blk, g] = jnp.concatenate([kprev, k2[g][rows]], axis=0)
            vals = jnp.concatenate([vtp, v_top[g][rows], vbp, v_bot[g][rows]], axis=0)
            vband[blk, g] = jnp.concatenate([vals, ones_cols], axis=1)
    o_pair = {}
    for w0 in range(0, len(inst), wave):
        wave_inst = inst[w0:w0 + wave]
        scores = {}
        for blk, g, pp in wave_inst:
            qp = q_pairs[g * n_pairs + pp][rows_of(blk)]
            zero = jnp.zeros_like(qp)
            lhs = jnp.concatenate([jnp.where(lo_blk, qp, zero), jnp.where(lo_blk, zero, qp)], axis=0)
            scores[blk, g, pp] = _dot_nt(lhs, kband[blk, g])
        p_cat, sink_term = {}, {}
        for blk, g, pp in wave_inst:
            pair = g * n_pairs + pp
            s = jnp.where(valid2[min(blk, 1)], scores[blk, g, pp], -jnp.inf)
            sink = jnp.where(row2 < WINDOW, sink_ref[2 * pair], sink_ref[2 * pair + 1])
            m = jnp.maximum(jnp.max(s, axis=-1, keepdims=True), sink)
            pb = jnp.exp(s - m).astype(BF16)
            p_cat[blk, g, pp] = jnp.concatenate([pb[:WINDOW], pb[WINDOW:]], axis=1)
            e = jnp.exp(sink - m)
            sink_term[blk, g, pp] = jnp.where(lo_blk, e[:WINDOW], e[WINDOW:])
        for i in wave_inst:
            pv = _dot(p_cat[i], vband[i[0], i[1]])
            o_pair[i] = pv[:, :LANES] / (pv[:, LANES:] + sink_term[i])
    o_all = jnp.concatenate(
        [jnp.concatenate([o_pair[blk, g, pp] for g in range(n_kv) for pp in range(n_pairs)], axis=1).astype(BF16)
         for blk in range(nblk)], axis=0)

    last = slice((nblk - 1) * WINDOW, nblk * WINDOW)
    for g in range(n_kv):
        k2_prev[g] = k2[g][last]
        v2_prev[g, 0] = v_top[g][last]
        v2_prev[g, 1] = v_bot[g][last]

    o_ref[...] = x + _dot(o_all, wo_ref[...]) + bo_ref[...]


def _attn(x, cos, sin, gain, wqkv, bqkv, qg, kg, sinks, wo, bo, ones_bd, *, bsz, tq=512, wave=8):
    n, d = x.shape
    t_len = n // bsz
    tq = min(tq, t_len)
    nt = t_len // tq
    n_q = wo.shape[0] // HEAD_DIM
    n_kv = (wqkv.shape[1] - wo.shape[0]) // (2 * HEAD_DIM)
    row = lambda b, t, *_: (b * nt + t, 0)
    grid_spec = pltpu.PrefetchScalarGridSpec(
        num_scalar_prefetch=1,
        grid=(bsz, nt),
        in_specs=[pl.BlockSpec((tq, d), row), pl.BlockSpec((tq, LANES), row), pl.BlockSpec((tq, LANES), row),
                  _resident((1, d)), _resident(wqkv.shape), _resident(bqkv.shape),
                  _resident((1, LANES)), _resident((1, LANES)), _resident(wo.shape), _resident(bo.shape),
                  _resident(ones_bd.shape)],
        out_specs=pl.BlockSpec((tq, d), row),
        scratch_shapes=[pltpu.VMEM((n_kv, WINDOW, LANES), BF16), pltpu.VMEM((n_kv, 2, WINDOW, LANES), BF16)],
    )
    return pl.pallas_call(
        functools.partial(_attn_body, n_q=n_q, n_kv=n_kv, wave=wave),
        grid_spec=grid_spec,
        out_shape=jax.ShapeDtypeStruct((n, d), F32),
        compiler_params=_cparams("parallel", "arbitrary"),
        name="swa_attention",
    )(sinks, x, cos, sin, gain, wqkv, bqkv, qg, kg, wo, bo, ones_bd)


def _mix_in_body(x_ref, xp_ref, gain_ref, win_ref, mu_ref, w0_ref, wdec_ref, a0_ref, waaa_ref, wgate_ref,
                 kk_ref, ka_ref, rk_ref, lng_ref, lnb_ref, wsp_ref, bsp_ref, ones_ref,
                 r_o, lw_o, k_o, v_o, a_o, b_o, gate_o, bonus_o, yb_o, *, a_width, b_width):
    t = pl.program_id(1)
    tm = x_ref.shape[0]
    a_cols = mu_ref.shape[1]
    gain = gain_ref[...]
    h = _rms(x_ref[...], gain).astype(BF16)
    hp = _rms(xp_ref[...], gain).astype(BF16)
    z = _dot(h, win_ref[...])
    zp = _dot(hp, win_ref[:, :a_cols])
    za = z[:, :a_cols]
    prev_row = jnp.where(t > 0, zp[xp_ref.shape[0] - 1:, :], 0.0)
    row = lax.broadcasted_iota(jnp.int32, (tm, 1), 0)
    shifted = jnp.where(row == 0, prev_row, pltpu.roll(za, 1, axis=0))
    za = za + (shifted - za) * mu_ref[...]

    aw = a_width
    r = za[:, :aw]
    k = za[:, aw:2 * aw]
    v = za[:, 2 * aw:3 * aw]
    xwa = za[:, 3 * aw:3 * aw + LANES]
    xg = za[:, 3 * aw + LANES:]
    y_dec = w0_ref[...] + _dot(jnp.tanh(xwa).astype(BF16), wdec_ref[...])
    lw = (-math.exp(-0.5)) * jax.nn.sigmoid(y_dec)
    iclr = jax.nn.sigmoid(a0_ref[...] + _dot(xwa.astype(BF16), waaa_ref[...]))
    gate = _dot(jax.nn.sigmoid(xg).astype(BF16), wgate_ref[...])
    ones_bd = ones_ref[...]
    kk = k * kk_ref[...]
    kk = kk / jnp.maximum(jnp.sqrt(_group_sum(kk * kk, ones_bd)), 1e-12)
    k = k * (1.0 + (iclr - 1.0) * ka_ref[...])
    r_o[...] = r
    lw_o[...] = lw
    k_o[...] = k
    v_o[...] = v
    a_o[...] = -kk
    b_o[...] = kk * iclr
    gate_o[...] = gate
    bonus_o[...] = _group_sum(r * k * rk_ref[...], ones_bd) * v

    zb = z[:, a_cols:]
    zb = 0.5 * zb * (1.0 + lax.erf(zb * math.sqrt(0.5)))
    u = zb[:, :b_width]
    s = zb[:, b_width:]
    mean = jnp.mean(s, axis=-1, keepdims=True)
    sc = s - mean
    var = jnp.mean(sc * sc, axis=-1, keepdims=True)
    s = (sc * lax.rsqrt(var + LN_EPS) * lng_ref[...] + lnb_ref[...]).astype(BF16)
    ti = lax.broadcasted_iota(jnp.int32, (WINDOW, WINDOW), 0)
    si = lax.broadcasted_iota(jnp.int32, (WINDOW, WINDOW), 1)
    n_groups = wsp_ref.shape[0]
    gd = b_width // n_groups
    for g in range(n_groups):
        w_g = jnp.where(si <= ti, wsp_ref[g], 0.0).astype(BF16)
        for c in range(tm // WINDOW):
            rows = slice(c * WINDOW, (c + 1) * WINDOW)
            cols = slice(g * gd, (g + 1) * gd)
            mixed = _dot(w_g, s[rows, cols]) + bsp_ref[:, cols]
            yb_o[rows, cols] = u[rows, cols] * mixed


def _mix_in(x, gain, win, mu, w0, wdec, a0, waaa, wgate, kk, ka, rk, lng, lnb, wsp, bsp, ones_bd, *, bsz, tm=512):
    n, d = x.shape
    t_len = n // bsz
    nt = t_len // tm
    a_width = w0.shape[1]
    b_width = lng.shape[1]
    prev_rows = 8
    row = lambda b, t: (b * nt + t, 0)
    prev = lambda b, t: (jnp.maximum((b * nt + t) * (tm // prev_rows) - 1, 0), 0)
    consts = [gain, win, mu, w0, wdec, a0, waaa, wgate, kk, ka, rk, lng, lnb, wsp, bsp, ones_bd]
    out_spec = pl.BlockSpec((tm, a_width), row)
    return pl.pallas_call(
        functools.partial(_mix_in_body, a_width=a_width, b_width=b_width),
        grid=(bsz, nt),
        in_specs=[pl.BlockSpec((tm, d), row), pl.BlockSpec((prev_rows, d), prev)] + [_resident(c.shape) for c in consts],
        out_specs=[out_spec] * 9,
        out_shape=[jax.ShapeDtypeStruct((n, a_width), F32)] * 9,
        compiler_params=_cparams("parallel", "parallel"),
        name="mix_in",
    )(x, x, *consts)


def _scan_body(r_ref, lw_ref, k_ref, v_ref, a_ref, b_ref, y_ref, state, *, n_heads):
    c_len = SCAN_CHUNK

    @pl.when(pl.program_id(1) == 0)
    def _():
        state[...] = jnp.zeros_like(state)

    ti = lax.broadcasted_iota(jnp.int32, (c_len, c_len), 0)
    si = lax.broadcasted_iota(jnp.int32, (c_len, c_len), 1)
    incl = si <= ti
    strict = si < ti
    tril = incl.astype(F32)
    eye = (si == ti).astype(F32)
    mid = c_len // 2 - 1

    n_chunks = r_ref.shape[0] // c_len
    heads = range(n_heads)
    hcols = [slice(hd * HEAD_DIM, (hd + 1) * HEAD_DIM) for hd in heads]

    pre = []
    for c in range(n_chunks):
        rows = slice(c * c_len, (c + 1) * c_len)
        lw = lw_ref[rows, :]
        cum = jnp.dot(tril, lw, preferred_element_type=F32, precision=lax.Precision.HIGHEST)
        l_mid = cum[mid:mid + 1, :]
        l_end = cum[c_len - 1:, :]
        e_neg = jnp.exp(l_mid - cum)
        e_pos = jnp.exp(cum - l_mid)
        e_end = jnp.exp(l_end - cum)
        pre.append(dict(
            rows=rows,
            p_mid=jnp.exp(l_mid), p_end=jnp.exp(l_end),
            a_t=a_ref[rows, :] * jnp.exp(cum - lw - l_mid),
            r_t=r_ref[rows, :] * e_pos,
            b_t=b_ref[rows, :] * e_neg,
            k_t=k_ref[rows, :] * e_neg,
            b_e=b_ref[rows, :] * e_end,
            k_e=k_ref[rows, :] * e_end,
            v=v_ref[rows, :]))

    inst = [(c, hd) for c in range(n_chunks) for hd in heads]
    lhs = {(c, hd): jnp.concatenate([pre[c]["a_t"][:, hcols[hd]], pre[c]["r_t"][:, hcols[hd]]], axis=0)
           for c, hd in inst}
    qk = {}
    for c, hd in inst:
        rhs = jnp.concatenate([pre[c]["b_t"][:, hcols[hd]], pre[c]["k_t"][:, hcols[hd]]], axis=0)
        qk[c, hd] = _dot_nt(lhs[c, hd].astype(BF16), rhs.astype(BF16))
    a_ab = {i: jnp.where(strict, qk[i][:c_len, :c_len], 0.0) for i in inst}
    a_rb = {i: jnp.where(incl, qk[i][c_len:, :c_len], 0.0).astype(BF16) for i in inst}
    a_xk = {i: jnp.concatenate([jnp.where(strict, qk[i][:c_len, c_len:], 0.0),
                                jnp.where(incl, qk[i][c_len:, c_len:], 0.0)], axis=0).astype(BF16) for i in inst}
    v_h = {(c, hd): pre[c]["v"][:, hcols[hd]].astype(BF16) for c, hd in inst}
    av = {i: _dot(a_xk[i], v_h[i]) for i in inst}
    inv = {i: eye + a_ab[i] for i in inst}
    q = {}
    for i in inst:
        qb = a_ab[i].astype(BF16)
        q[i] = _dot(qb, qb)
    covered = 2
    while 2 * covered < c_len:
        for i in inst:
            qb = q[i].astype(BF16)
            prod = _dot(qb, jnp.concatenate([qb, inv[i].astype(BF16)], axis=1))
            q[i] = prod[:, :c_len]
            inv[i] = inv[i] + prod[:, c_len:]
        covered *= 2
    for i in inst:
        inv[i] = (inv[i] + _dot(q[i].astype(BF16), inv[i].astype(BF16))).astype(BF16)
    bk = {(c, hd): jnp.concatenate([pre[c]["b_e"][:, hcols[hd]], pre[c]["k_e"][:, hcols[hd]]], axis=0).astype(BF16)
          for c, hd in inst}
    lhs_s = {(c, hd): (lhs[c, hd] * pre[c]["p_mid"][:, hcols[hd]]).astype(BF16) for c, hd in inst}

    s_cur = [state[hd] for hd in heads]
    for c in range(n_chunks):
        sa = [_dot_nt(lhs_s[c, hd], s_cur[hd].astype(BF16)) for hd in heads]
        u = [_dot(inv[c, hd], (sa[hd][:c_len] + av[c, hd][:c_len]).astype(BF16)) for hd in heads]
        for hd in heads:
            uv = jnp.concatenate([u[hd].astype(BF16), v_h[c, hd]], axis=0)
            s_cur[hd] = s_cur[hd] * pre[c]["p_end"][:, hcols[hd]] + _dot_tn(uv, bk[c, hd])
        for hd in heads:
            y = sa[hd][c_len:] + av[c, hd][c_len:] + _dot(a_rb[c, hd], u[hd].astype(BF16))
            y_ref[pre[c]["rows"], hcols[hd]] = y
    for hd in heads:
        state[hd] = s_cur[hd]


def _scan(r, lw, k, v, a, b, *, bsz, ts=256):
    n, width = r.shape
    nt = n // bsz // ts
    n_heads = width // HEAD_DIM
    spec = pl.BlockSpec((ts, width), lambda bb, t: (bb * nt + t, 0))
    return pl.pallas_call(
        functools.partial(_scan_body, n_heads=n_heads),
        grid=(bsz, nt),
        in_specs=[spec] * 6,
        out_specs=spec,
        out_shape=jax.ShapeDtypeStruct((n, width), F32),
        scratch_shapes=[pltpu.VMEM((n_heads, HEAD_DIM, HEAD_DIM), F32)],
        compiler_params=_cparams("parallel", "arbitrary"),
        name="rwkv_scan",
    )(r, lw, k, v, a, b)


def _mix_out_body(x_ref, y_ref, bonus_ref, gate_ref, yb_ref, gng_ref, gnb_ref, ones_ref, wa_ref, wb_ref, o_ref):
    ones_bd = ones_ref[...]
    y = y_ref[...]
    mu = _group_sum(y, ones_bd) * (1.0 / HEAD_DIM)
    yc = y - mu
    var = _group_sum(yc * yc, ones_bd) * (1.0 / HEAD_DIM)
    yn = yc * lax.rsqrt(var + GN_EPS) * gng_ref[...] + gnb_ref[...]
    ya = ((yn + bonus_ref[...]) * gate_ref[...]).astype(BF16)
    o_ref[...] = x_ref[...] + _dot(ya, wa_ref[...]) + _dot(yb_ref[...].astype(BF16), wb_ref[...])


def _mix_out(x, y, bonus, gate, yb, gng, gnb, ones_bd, wa, wb, *, tm=512):
    n, d = x.shape
    w = y.shape[1]
    xs = pl.BlockSpec((tm, d), lambda i: (i, 0))
    ws = pl.BlockSpec((tm, w), lambda i: (i, 0))
    consts = [gng, gnb, ones_bd, wa, wb]
    return pl.pallas_call(
        _mix_out_body,
        grid=(n // tm,),
        in_specs=[xs, ws, ws, ws, ws] + [_resident(c.shape) for c in consts],
        out_specs=xs,
        out_shape=jax.ShapeDtypeStruct((n, d), F32),
        compiler_params=_cparams("parallel"),
        name="mix_out",
    )(x, y, bonus, gate, yb, *consts)


def _row(v):
    return v.reshape(1, -1)


def kernel(x, positions, ffn1_norm, ffn1_w_gate, ffn1_w_up, ffn1_w_down, mix_norm, ffn2_norm, ffn2_w_gate, ffn2_w_up, ffn2_w_down, ab_w_in, ab_mu_shift, rwkv_w0, rwkv_w_decay_up, rwkv_a0, rwkv_w_aaa_up, rwkv_w_gate_up, rwkv_k_k, rwkv_k_a, rwkv_r_k, rwkv_gn_gain, rwkv_gn_bias, sg_ln_gain, sg_ln_bias, sg_w_spatial, sg_b_spatial, ab_w_out, attn_w_qkv, attn_b_qkv, attn_q_norm, attn_k_norm, attn_sinks, attn_w_o, attn_b_o):
    bsz, t_len, d = x.shape
    depth = ffn1_norm.shape[0]
    a_width = rwkv_w0.shape[1]
    b_width = sg_ln_gain.shape[1]
    n_groups = sg_w_spatial.shape[1]
    bf = lambda w: w.astype(BF16)

    xf = x.reshape(bsz * t_len, d)
    cos, sin = _rope_tables(positions.reshape(-1, 1))
    head_id = jnp.arange(a_width) // HEAD_DIM
    ones_bd = (head_id[:, None] == head_id[None, :]).astype(BF16)
    pairs = LANES // HEAD_DIM

    for l in range(depth):
        i = l // 2
        xf = _ffn(xf, _row(ffn1_norm[l]), bf(ffn1_w_gate[l]), bf(ffn1_w_up[l]), bf(ffn1_w_down[l]))
        if l % 2 == 0:
            lora = rwkv_w_decay_up.shape[1]
            zeros = jnp.zeros((LANES - lora, a_width), F32)
            wdec = bf(jnp.concatenate([rwkv_w_decay_up[i], zeros], axis=0))
            waaa = bf(jnp.concatenate([zeros, rwkv_w_aaa_up[i]], axis=0))
            bsp = jnp.repeat(sg_b_spatial[i].T, b_width // n_groups, axis=1)
            r, lw, k, v, a, b, gate, bonus, yb = _mix_in(
                xf, _row(mix_norm[l]), bf(ab_w_in[i]), _row(ab_mu_shift[i]), _row(rwkv_w0[i]), wdec,
                _row(rwkv_a0[i]), waaa, bf(rwkv_w_gate_up[i]), _row(rwkv_k_k[i]), _row(rwkv_k_a[i]),
                _row(rwkv_r_k[i]), _row(sg_ln_gain[i]), _row(sg_ln_bias[i]), sg_w_spatial[i], bsp, ones_bd,
                bsz=bsz)
            y = _scan(r, lw, k, v, a, b, bsz=bsz)
            xf = _mix_out(xf, y, bonus, gate, yb, _row(rwkv_gn_gain[i]), _row(rwkv_gn_bias[i]), ones_bd,
                          bf(ab_w_out[i][:a_width]), bf(ab_w_out[i][a_width:]))
        else:
            xf = _attn(xf, cos, sin, _row(mix_norm[l]), bf(attn_w_qkv[i]), _row(attn_b_qkv[i]),
                       _row(jnp.tile(attn_q_norm[i], pairs)), _row(jnp.tile(attn_k_norm[i], pairs)),
                       attn_sinks[i], bf(attn_w_o[i]), _row(attn_b_o[i]), ones_bd[:2 * LANES, :2 * LANES], bsz=bsz)
        xf = _ffn(xf, _row(ffn2_norm[l]), bf(ffn2_w_gate[l]), bf(ffn2_w_up[l]), bf(ffn2_w_down[l]))
    return xf.reshape(bsz, t_len, d)
```

```python
import functools
import math
from typing import NamedTuple

import jax
import jax.numpy as jnp
from jax import lax
from jax.experimental import pallas as pl
from jax.experimental.pallas import tpu as pltpu

F32 = jnp.float32
BF16 = jnp.bfloat16

HEAD_DIM = 64
RMS_EPS = 1e-6
GN_EPS = 64e-5
LN_EPS = 1e-5
ROPE_THETA = 10000.0
WINDOW = 128
SCAN_CHUNK = 64
LANES = 128
VMEM_LIMIT = 56 * 1024 * 1024


def _cparams(*sem):
    return pltpu.CompilerParams(dimension_semantics=sem, vmem_limit_bytes=VMEM_LIMIT)


def _resident(shape):
    nd = len(shape)
    return pl.BlockSpec(shape, lambda *_: (0,) * nd, pipeline_mode=pl.Buffered(1))


class _LayerSlice(NamedTuple):
    array: jax.Array
    shape: tuple
    index: tuple


def _layer(array, layer, rows=None, row_block=0):
    shape = tuple(array.shape[1:])
    if rows is not None:
        shape = (rows,) + shape[1:]
    return _LayerSlice(array, shape, (layer, row_block, 0))


def _const_spec(c):
    if isinstance(c, _LayerSlice):
        return pl.BlockSpec((None,) + c.shape, lambda *_: c.index, pipeline_mode=pl.Buffered(1))
    return _resident(c.shape)


def _const_arg(c):
    return c.array if isinstance(c, _LayerSlice) else c


def _dot(a, b):
    return jnp.dot(a, b, preferred_element_type=F32)


def _dot_nt(a, b):
    return lax.dot_general(a, b, (((1,), (1,)), ((), ())), preferred_element_type=F32)


def _dot_tn(a, b):
    return lax.dot_general(a, b, (((0,), (0,)), ((), ())), preferred_element_type=F32)


def _rms(x, gain):
    return x * lax.rsqrt(jnp.mean(x * x, axis=-1, keepdims=True) + RMS_EPS) * gain


def _split_bf16(x):
    hi = x.astype(BF16)
    lo = (x - hi.astype(F32)).astype(BF16)
    return hi, lo


def _group_sum(x, ones_bd):
    hi, lo = _split_bf16(x)
    bd = ones_bd.shape[0]
    slabs = [_dot(hi[:, c:c + bd], ones_bd) + _dot(lo[:, c:c + bd], ones_bd) for c in range(0, x.shape[1], bd)]
    return slabs[0] if len(slabs) == 1 else jnp.concatenate(slabs, axis=1)


def _ffn_body(x_ref, gain_ref, wg_ref, wu_ref, wd_ref, o_ref, *, n_chunks):
    x = x_ref[...]
    h = _rms(x, gain_ref[...]).astype(BF16)
    fc = wg_ref.shape[1] // n_chunks
    acc = jnp.zeros_like(x)
    for c in range(n_chunks):
        g = _dot(h, wg_ref[:, c * fc:(c + 1) * fc])
        u = _dot(h, wu_ref[:, c * fc:(c + 1) * fc])
        act = (g * jax.nn.sigmoid(g) * u).astype(BF16)
        acc = acc + _dot(act, wd_ref[c * fc:(c + 1) * fc, :])
    o_ref[...] = x + 0.5 * acc


def _ffn(x, gain, wg, wu, wd, *, tm=1024, n_chunks=11):
    n, d = x.shape
    tm = min(tm, n)
    return pl.pallas_call(
        functools.partial(_ffn_body, n_chunks=n_chunks),
        grid=(n // tm,),
        in_specs=[pl.BlockSpec((tm, d), lambda i: (i, 0))] + [_const_spec(c) for c in (gain, wg, wu, wd)],
        out_specs=pl.BlockSpec((tm, d), lambda i: (i, 0)),
        out_shape=jax.ShapeDtypeStruct((n, d), F32),
        compiler_params=_cparams("parallel"),
        name="ffn",
    )(x, *map(_const_arg, (gain, wg, wu, wd)))


def _rope_body(pos_ref, freq_ref, cos_ref, sin_ref):
    ang = pos_ref[...].astype(F32) * freq_ref[...]
    lane = lax.broadcasted_iota(jnp.int32, ang.shape, 1)
    first_half = (lane % HEAD_DIM) < HEAD_DIM // 2
    cos_ref[...] = jnp.cos(ang)
    s = jnp.sin(ang)
    sin_ref[...] = jnp.where(first_half, -s, s)


def _rope_tables(pos, *, tm=2048):
    n = pos.shape[0]
    tm = min(tm, n)
    half = HEAD_DIM // 2
    inv_freq = ROPE_THETA ** (-jnp.arange(0, HEAD_DIM, 2, dtype=F32) / HEAD_DIM)
    freq = jnp.tile(inv_freq, LANES // half)[None, :]
    return pl.pallas_call(
        _rope_body,
        grid=(n // tm,),
        in_specs=[pl.BlockSpec((tm, 1), lambda i: (i, 0)), _resident((1, LANES))],
        out_specs=[pl.BlockSpec((tm, LANES), lambda i: (i, 0))] * 2,
        out_shape=[jax.ShapeDtypeStruct((n, LANES), F32)] * 2,
        compiler_params=_cparams("parallel"),
        name="rope_tables",
    )(pos, freq)


def _attn_body(sink_ref, x_ref, cos_ref, sin_ref, gain_ref, wqkv_ref, bqkv_ref, qg_ref, kg_ref, wo_ref, bo_ref,
               ones_ref, o_ref, k2_prev, v2_prev, *, n_q, n_kv, wave):
    t = pl.program_id(1)
    tq = x_ref.shape[0]
    nblk = tq // WINDOW
    qw = n_q * HEAD_DIM
    kw = n_kv * HEAD_DIM
    per_kv = n_q // n_kv

    @pl.when(t == 0)
    def _():
        k2_prev[...] = jnp.zeros_like(k2_prev)
        v2_prev[...] = jnp.zeros_like(v2_prev)

    x = x_ref[...]
    h = _rms(x, gain_ref[...]).astype(BF16)
    qkv = _dot(h, wqkv_ref[...]) + bqkv_ref[...]

    lane = lax.broadcasted_iota(jnp.int32, (tq, LANES), 1)
    lo = lane < HEAD_DIM
    first_half = (lane % HEAD_DIM) < HEAD_DIM // 2
    cos = cos_ref[...]
    sin = sin_ref[...]

    ones_bd = ones_ref[...]

    def norm_rope(z, gain_pair, extra):
        inv = lax.rsqrt(_group_sum(z * z, ones_bd) * (1.0 / HEAD_DIM) + RMS_EPS) * extra
        out = []
        for p in range(z.shape[1] // LANES):
            pc = slice(p * LANES, (p + 1) * LANES)
            zg = z[:, pc] * gain_pair
            partner = jnp.where(first_half, pltpu.roll(zg, LANES - HEAD_DIM // 2, axis=1),
                                pltpu.roll(zg, HEAD_DIM // 2, axis=1))
            out.append((zg * cos + partner * sin) * inv[:, pc])
        return out

    scale = HEAD_DIM ** -0.5
    q_pairs = [qp.astype(BF16) for qp in norm_rope(qkv[:, :qw], qg_ref[...], scale)]
    k_pairs = norm_rope(qkv[:, qw:qw + kw], kg_ref[...], 1.0)
    v_pairs = [qkv[:, qw + kw + p * LANES: qw + kw + (p + 1) * LANES] for p in range(kw // LANES)]

    k2, v_top, v_bot = [], [], []
    for g in range(n_kv):
        kp = k_pairs[g // 2]
        vp = v_pairs[g // 2]
        kr = pltpu.roll(kp, HEAD_DIM, axis=1)
        vr = pltpu.roll(vp, HEAD_DIM, axis=1)
        if g % 2 == 0:
            k2.append(jnp.where(lo, kp, kr).astype(BF16))
            v_top.append(jnp.where(lo, vp, 0.0).astype(BF16))
            v_bot.append(jnp.where(lo, 0.0, vr).astype(BF16))
        else:
            k2.append(jnp.where(lo, kr, kp).astype(BF16))
            v_top.append(jnp.where(lo, vr, 0.0).astype(BF16))
            v_bot.append(jnp.where(lo, 0.0, vp).astype(BF16))

    qi = lax.broadcasted_iota(jnp.int32, (WINDOW, 2 * WINDOW), 0)
    kj = lax.broadcasted_iota(jnp.int32, (WINDOW, 2 * WINDOW), 1)
    dist = qi + WINDOW - kj
    in_window = (dist >= 0) & (dist < WINDOW)
    lo_blk = lax.broadcasted_iota(jnp.int32, (WINDOW, LANES), 1) < HEAD_DIM
    oc_row = lax.broadcasted_iota(jnp.int32, (4 * WINDOW, LANES), 0)
    oc_lane = lax.broadcasted_iota(jnp.int32, (4 * WINDOW, LANES), 1)
    ones_cols = ((oc_row < 2 * WINDOW) == (oc_lane < HEAD_DIM)).astype(BF16)

    n_pairs = per_kv // 2
    inst = [(blk, g, pp) for blk in range(nblk) for g in range(n_kv) for pp in range(n_pairs)]
    rows_of = lambda blk: slice(blk * WINDOW, (blk + 1) * WINDOW)
    row2 = lax.broadcasted_iota(jnp.int32, (2 * WINDOW, 1), 0)
    valid_first = in_window & (kj >= jnp.where(t > 0, 0, WINDOW))
    valid2 = {0: jnp.concatenate([valid_first, valid_first], axis=0)}
    if nblk > 1:
        valid2[1] = jnp.concatenate([in_window, in_window], axis=0)
    kband, vband = {}, {}
    for blk in range(nblk):
        for g in range(n_kv):
            if blk == 0:
                kprev, vtp, vbp = k2_prev[g], v2_prev[g, 0], v2_prev[g, 1]
            else:
                prows = rows_of(blk - 1)
                kprev, vtp, vbp = k2[g][prows], v_top[g][prows], v_bot[g][prows]
            rows = rows_of(blk)
            kband[blk, g] = jnp.concatenate([kprev, k2[g][rows]], axis=0)
            vals = jnp.concatenate([vtp, v_top[g][rows], vbp, v_bot[g][rows]], axis=0)
            vband[blk, g] = jnp.concatenate([vals, ones_cols], axis=1)
    o_pair = {}
    for w0 in range(0, len(inst), wave):
        wave_inst = inst[w0:w0 + wave]
        scores = {}
        for blk, g, pp in wave_inst:
            qp = q_pairs[g * n_pairs + pp][rows_of(blk)]
            zero = jnp.zeros_like(qp)
            lhs = jnp.concatenate([jnp.where(lo_blk, qp, zero), jnp.where(lo_blk, zero, qp)], axis=0)
            scores[blk, g, pp] = _dot_nt(lhs, kband[blk, g])
        p_cat, sink_term = {}, {}
        for blk, g, pp in wave_inst:
            pair = g * n_pairs + pp
            s = jnp.where(valid2[min(blk, 1)], scores[blk, g, pp], -jnp.inf)
            sink = jnp.where(row2 < WINDOW, sink_ref[2 * pair], sink_ref[2 * pair + 1])
            m = jnp.maximum(jnp.max(s, axis=-1, keepdims=True), sink)
            pb = jnp.exp(s - m).astype(BF16)
            p_cat[blk, g, pp] = jnp.concatenate([pb[:WINDOW], pb[WINDOW:]], axis=1)
            e = jnp.exp(sink - m)
            sink_term[blk, g, pp] = jnp.where(lo_blk, e[:WINDOW], e[WINDOW:])
        for i in wave_inst:
            pv = _dot(p_cat[i], vband[i[0], i[1]])
            o_pair[i] = pv[:, :LANES] / (pv[:, LANES:] + sink_term[i])
    o_all = jnp.concatenate(
        [jnp.concatenate([o_pair[blk, g, pp] for g in range(n_kv) for pp in range(n_pairs)], axis=1).astype(BF16)
         for blk in range(nblk)], axis=0)

    last = slice((nblk - 1) * WINDOW, nblk * WINDOW)
    for g in range(n_kv):
        k2_prev[g] = k2[g][last]
        v2_prev[g, 0] = v_top[g][last]
        v2_prev[g, 1] = v_bot[g][last]

    o_ref[...] = x + _dot(o_all, wo_ref[...]) + bo_ref[...]


def _attn(x, cos, sin, gain, wqkv, bqkv, qg, kg, sinks, wo, bo, ones_bd, *, bsz, tq=512, wave=8):
    n, d = x.shape
    t_len = n // bsz
    tq = min(tq, t_len)
    nt = t_len // tq
    n_q = wo.shape[0] // HEAD_DIM
    n_kv = (wqkv.shape[1] - wo.shape[0]) // (2 * HEAD_DIM)
    row = lambda b, t, *_: (b * nt + t, 0)
    consts = (gain, wqkv, bqkv, qg, kg, wo, bo, ones_bd)
    grid_spec = pltpu.PrefetchScalarGridSpec(
        num_scalar_prefetch=1,
        grid=(bsz, nt),
        in_specs=[pl.BlockSpec((tq, d), row), pl.BlockSpec((tq, LANES), row), pl.BlockSpec((tq, LANES), row)]
        + [_const_spec(c) for c in consts],
        out_specs=pl.BlockSpec((tq, d), row),
        scratch_shapes=[pltpu.VMEM((n_kv, WINDOW, LANES), BF16), pltpu.VMEM((n_kv, 2, WINDOW, LANES), BF16)],
    )
    return pl.pallas_call(
        functools.partial(_attn_body, n_q=n_q, n_kv=n_kv, wave=wave),
        grid_spec=grid_spec,
        out_shape=jax.ShapeDtypeStruct((n, d), F32),
        compiler_params=_cparams("parallel", "arbitrary"),
        name="swa_attention",
    )(sinks, x, cos, sin, *map(_const_arg, consts))


def _mix_in_body(x_ref, xp_ref, gain_ref, win_ref, mu_ref, w0_ref, wdec_ref, a0_ref, waaa_ref, wgate_ref,
                 kk_ref, ka_ref, rk_ref, lng_ref, lnb_ref, wsp_ref, bsp_ref, ones_ref,
                 r_o, lw_o, k_o, v_o, a_o, b_o, gate_o, bonus_o, yb_o, *, a_width, b_width):
    t = pl.program_id(1)
    tm = x_ref.shape[0]
    a_cols = mu_ref.shape[1]
    gain = gain_ref[...]
    h = _rms(x_ref[...], gain).astype(BF16)
    hp = _rms(xp_ref[...], gain).astype(BF16)
    z = _dot(h, win_ref[...])
    zp = _dot(hp, win_ref[:, :a_cols])
    za = z[:, :a_cols]
    prev_row = jnp.where(t > 0, zp[xp_ref.shape[0] - 1:, :], 0.0)
    row = lax.broadcasted_iota(jnp.int32, (tm, 1), 0)
    shifted = jnp.where(row == 0, prev_row, pltpu.roll(za, 1, axis=0))
    za = za + (shifted - za) * mu_ref[...]

    aw = a_width
    r = za[:, :aw]
    k = za[:, aw:2 * aw]
    v = za[:, 2 * aw:3 * aw]
    xwa = za[:, 3 * aw:3 * aw + LANES]
    xg = za[:, 3 * aw + LANES:]
    y_dec = w0_ref[...] + _dot(jnp.tanh(xwa).astype(BF16), wdec_ref[...])
    lw = (-math.exp(-0.5)) * jax.nn.sigmoid(y_dec)
    iclr = jax.nn.sigmoid(a0_ref[...] + _dot(xwa.astype(BF16), waaa_ref[...]))
    gate = _dot(jax.nn.sigmoid(xg).astype(BF16), wgate_ref[...])
    ones_bd = ones_ref[...]
    kk = k * kk_ref[...]
    kk = kk / jnp.maximum(jnp.sqrt(_group_sum(kk * kk, ones_bd)), 1e-12)
    k = k * (1.0 + (iclr - 1.0) * ka_ref[...])
    r_o[...] = r
    lw_o[...] = lw
    k_o[...] = k
    v_o[...] = v
    a_o[...] = -kk
    b_o[...] = kk * iclr
    gate_o[...] = gate
    bonus_o[...] = _group_sum(r * k * rk_ref[...], ones_bd) * v

    zb = z[:, a_cols:]
    zb = 0.5 * zb * (1.0 + lax.erf(zb * math.sqrt(0.5)))
    u = zb[:, :b_width]
    s = zb[:, b_width:]
    mean = jnp.mean(s, axis=-1, keepdims=True)
    sc = s - mean
    var = jnp.mean(sc * sc, axis=-1, keepdims=True)
    s = (sc * lax.rsqrt(var + LN_EPS) * lng_ref[...] + lnb_ref[...]).astype(BF16)
    ti = lax.broadcasted_iota(jnp.int32, (WINDOW, WINDOW), 0)
    si = lax.broadcasted_iota(jnp.int32, (WINDOW, WINDOW), 1)
    n_groups = wsp_ref.shape[0]
    gd = b_width // n_groups
    for g in range(n_groups):
        w_g = jnp.where(si <= ti, wsp_ref[g], 0.0).astype(BF16)
        for c in range(tm // WINDOW):
            rows = slice(c * WINDOW, (c + 1) * WINDOW)
            cols = slice(g * gd, (g + 1) * gd)
            mixed = _dot(w_g, s[rows, cols]) + bsp_ref[:, cols]
            yb_o[rows, cols] = (u[rows, cols] * mixed).astype(yb_o.dtype)


def _mix_in(x, gain, win, mu, w0, wdec, a0, waaa, wgate, kk, ka, rk, lng, lnb, wsp, bsp, ones_bd, *, bsz, tm=512):
    n, d = x.shape
    t_len = n // bsz
    nt = t_len // tm
    a_width = w0.shape[1]
    b_width = lng.shape[1]
    prev_rows = 8
    row = lambda b, t: (b * nt + t, 0)
    prev = lambda b, t: (jnp.maximum((b * nt + t) * (tm // prev_rows) - 1, 0), 0)
    consts = [gain, win, mu, w0, wdec, a0, waaa, wgate, kk, ka, rk, lng, lnb, wsp, bsp, ones_bd]
    out_spec = pl.BlockSpec((tm, a_width), row)
    return pl.pallas_call(
        functools.partial(_mix_in_body, a_width=a_width, b_width=b_width),
        grid=(bsz, nt),
        in_specs=[pl.BlockSpec((tm, d), row), pl.BlockSpec((prev_rows, d), prev)] + [_const_spec(c) for c in consts],
        out_specs=[out_spec] * 9,
        out_shape=[jax.ShapeDtypeStruct((n, a_width), F32)] * 8 + [jax.ShapeDtypeStruct((n, b_width), BF16)],
        compiler_params=_cparams("parallel", "parallel"),
        name="mix_in",
    )(x, x, *map(_const_arg, consts))


def _scan_body(r_ref, lw_ref, k_ref, v_ref, a_ref, b_ref, y_ref, state, *, n_heads):
    c_len = SCAN_CHUNK
    assert c_len == HEAD_DIM and 2 * HEAD_DIM == LANES

    @pl.when(pl.program_id(1) == 0)
    def _():
        state[...] = jnp.zeros_like(state)

    two_c = 2 * c_len
    ti = lax.broadcasted_iota(jnp.int32, (c_len, c_len), 0)
    si = lax.broadcasted_iota(jnp.int32, (c_len, c_len), 1)
    tril = (si <= ti).astype(F32)
    mid = c_len // 2 - 1
    row2 = lax.broadcasted_iota(jnp.int32, (two_c, two_c), 0)
    col2 = lax.broadcasted_iota(jnp.int32, (two_c, two_c), 1)
    t_idx, s_idx = row2 % c_len, col2 % c_len
    tri = (s_idx < t_idx) | ((row2 >= c_len) & (s_idx == t_idx))
    anti_eye = (t_idx == s_idx).astype(F32)
    lo2 = col2 < HEAD_DIM
    same_head = (row2 < HEAD_DIM) == lo2
    lo1 = lax.broadcasted_iota(jnp.int32, (c_len, LANES), 1) < HEAD_DIM

    n_seq, ts, _ = r_ref.shape
    n_chunks = ts // c_len
    n_pairs = n_heads * HEAD_DIM // LANES
    pcols = [slice(p * LANES, (p + 1) * LANES) for p in range(n_pairs)]

    pre = {}
    for c in range(n_chunks):
        rows = slice(c * c_len, (c + 1) * c_len)
        for sq in range(n_seq):
            lw = lw_ref[sq, rows, :]
            cum = jnp.dot(tril, lw, preferred_element_type=F32, precision=lax.Precision.HIGHEST)
            l_mid = cum[mid:mid + 1, :]
            l_end = cum[c_len - 1:, :]
            e_neg = jnp.exp(l_mid - cum)
            e_pos = jnp.exp(cum - l_mid)
            e_end = jnp.exp(l_end - cum)
            pre[c, sq] = dict(
                rows=rows,
                p_mid=jnp.exp(l_mid), p_end=jnp.exp(l_end),
                a_t=a_ref[sq, rows, :] * jnp.exp(cum - lw - l_mid),
                r_t=r_ref[sq, rows, :] * e_pos,
                b_t=b_ref[sq, rows, :] * e_neg,
                k_t=k_ref[sq, rows, :] * e_neg,
                b_e=b_ref[sq, rows, :] * e_end,
                k_e=k_ref[sq, rows, :] * e_end,
                v=v_ref[sq, rows, :])

    groups = [(sq, p) for sq in range(n_seq) for p in range(n_pairs)]
    inst = [(c, sq, p) for c in range(n_chunks) for sq, p in groups]
    lhs_s, qkm, v_b, bk = {}, {}, {}, {}
    for c, sq, p in inst:
        pc, pr = pcols[p], pre[c, sq]
        lhs = jnp.concatenate([pr["a_t"][:, pc], pr["r_t"][:, pc]], axis=0)
        lhs_s[c, sq, p] = (lhs * pr["p_mid"][:, pc]).astype(BF16)
        lhs_b = lhs.astype(BF16)
        bt = pr["b_t"][:, pc].astype(BF16)
        kt = pr["k_t"][:, pc].astype(BF16)
        qk0 = _dot_nt(jnp.where(lo2, lhs_b, 0), jnp.concatenate([kt, bt], axis=0))
        qk1 = _dot_nt(jnp.where(lo2, 0, lhs_b), jnp.concatenate([bt, kt], axis=0))
        qkm[c, sq, p] = (jnp.where(tri, qk0, 0.0), jnp.where(tri, qk1, 0.0))
        v_b[c, sq, p] = pr["v"][:, pc].astype(BF16)
        bk[c, sq, p] = jnp.concatenate([pr["b_e"][:, pc], pr["k_e"][:, pc]], axis=0).astype(BF16)
    a_rb, av, work = {}, {}, {}
    for i in inst:
        q0, q1 = qkm[i]
        a_rb[i] = jnp.where(lo1, q1[c_len:], q0[c_len:]).astype(BF16)
        v2 = jnp.concatenate([jnp.where(lo1, v_b[i], 0), jnp.where(lo1, 0, v_b[i])], axis=0)
        av[i] = _dot(jnp.where(lo2, q0, q1).astype(BF16), v2)
        work[i] = jnp.where(same_head, jnp.concatenate([q1[:c_len], q0[:c_len]], axis=0), anti_eye)
    for _ in range(c_len.bit_length() - 1):
        for i in inst:
            wb = work[i].astype(BF16)
            prod = _dot(jnp.where(same_head, wb, 0), wb)
            work[i] = jnp.where(same_head, prod, work[i] + prod)
    inv = {i: jnp.where(same_head, 0.0, work[i]).astype(BF16) for i in inst}

    s_cur = {g: state[gi] for gi, g in enumerate(groups)}
    for c in range(n_chunks):
        sa = {g: _dot_nt(lhs_s[(c,) + g], s_cur[g].astype(BF16)) for g in groups}
        u2 = {}
        for g in groups:
            x = (sa[g][:c_len] + av[(c,) + g][:c_len]).astype(BF16)
            rhs = jnp.concatenate([jnp.where(lo1, x, 0), jnp.where(lo1, 0, x)], axis=0)
            u2[g] = _dot(inv[(c,) + g], rhs)
        for g in groups:
            i = (c,) + g
            uv = jnp.concatenate([(u2[g][:c_len] + u2[g][c_len:]).astype(BF16), v_b[i]], axis=0)
            grown = jnp.where(same_head, _dot_tn(uv, bk[i]), 0.0)
            s_cur[g] = s_cur[g] * pre[c, g[0]]["p_end"][:, pcols[g[1]]] + grown
        for g in groups:
            i = (c,) + g
            y = sa[g][c_len:] + av[i][c_len:] + _dot(a_rb[i], u2[g].astype(BF16))
            y_ref[g[0], pre[c, g[0]]["rows"], pcols[g[1]]] = y
    for gi, g in enumerate(groups):
        state[gi] = s_cur[g]


def _scan(r, lw, k, v, a, b, *, bsz, ts=256, n_seq=4):
    n, width = r.shape
    t_len = n // bsz
    ts = min(ts, t_len)
    n_seq = math.gcd(n_seq, bsz)
    n_heads = width // HEAD_DIM
    spec = pl.BlockSpec((n_seq, ts, width), lambda bb, t: (bb, t, 0))
    y = pl.pallas_call(
        functools.partial(_scan_body, n_heads=n_heads),
        grid=(bsz // n_seq, t_len // ts),
        in_specs=[spec] * 6,
        out_specs=spec,
        out_shape=jax.ShapeDtypeStruct((bsz, t_len, width), F32),
        scratch_shapes=[pltpu.VMEM((n_seq * width // LANES, LANES, LANES), F32)],
        compiler_params=_cparams("parallel", "arbitrary"),
        name="rwkv_scan",
    )(*(z.reshape(bsz, t_len, width) for z in (r, lw, k, v, a, b)))
    return y.reshape(n, width)


def _mix_out_body(x_ref, y_ref, bonus_ref, gate_ref, yb_ref, gng_ref, gnb_ref, ones_ref, wa_ref, wb_ref, o_ref):
    ones_bd = ones_ref[...]
    y = y_ref[...]
    mu = _group_sum(y, ones_bd) * (1.0 / HEAD_DIM)
    yc = y - mu
    var = _group_sum(yc * yc, ones_bd) * (1.0 / HEAD_DIM)
    yn = yc * lax.rsqrt(var + GN_EPS) * gng_ref[...] + gnb_ref[...]
    ya = ((yn + bonus_ref[...]) * gate_ref[...]).astype(BF16)
    o_ref[...] = x_ref[...] + _dot(ya, wa_ref[...]) + _dot(yb_ref[...], wb_ref[...])


def _mix_out(x, y, bonus, gate, yb, gng, gnb, ones_bd, wa, wb, *, tm=512):
    n, d = x.shape
    w = y.shape[1]
    xs = pl.BlockSpec((tm, d), lambda i: (i, 0))
    ws = pl.BlockSpec((tm, w), lambda i: (i, 0))
    consts = [gng, gnb, ones_bd, wa, wb]
    return pl.pallas_call(
        _mix_out_body,
        grid=(n // tm,),
        in_specs=[xs, ws, ws, ws, ws] + [_const_spec(c) for c in consts],
        out_specs=xs,
        out_shape=jax.ShapeDtypeStruct((n, d), F32),
        compiler_params=_cparams("parallel"),
        name="mix_out",
    )(x, y, bonus, gate, yb, *map(_const_arg, consts))


def _row(v):
    return v.reshape(1, -1)


def kernel(x, positions, ffn1_norm, ffn1_w_gate, ffn1_w_up, ffn1_w_down, mix_norm, ffn2_norm, ffn2_w_gate, ffn2_w_up, ffn2_w_down, ab_w_in, ab_mu_shift, rwkv_w0, rwkv_w_decay_up, rwkv_a0, rwkv_w_aaa_up, rwkv_w_gate_up, rwkv_k_k, rwkv_k_a, rwkv_r_k, rwkv_gn_gain, rwkv_gn_bias, sg_ln_gain, sg_ln_bias, sg_w_spatial, sg_b_spatial, ab_w_out, attn_w_qkv, attn_b_qkv, attn_q_norm, attn_k_norm, attn_sinks, attn_w_o, attn_b_o):
    bsz, t_len, d = x.shape
    depth = ffn1_norm.shape[0]
    a_width = rwkv_w0.shape[1]
    b_width = sg_ln_gain.shape[1]
    n_groups = sg_w_spatial.shape[1]
    bf = lambda w: w.astype(BF16)

    xf = x.reshape(bsz * t_len, d)
    cos, sin = _rope_tables(positions.reshape(-1, 1))
    head_id = jnp.arange(2 * LANES) // HEAD_DIM
    ones_bd = (head_id[:, None] == head_id[None, :]).astype(BF16)
    pairs = LANES // HEAD_DIM
    ffn1 = [bf(w) for w in (ffn1_w_gate, ffn1_w_up, ffn1_w_down)]
    ffn2 = [bf(w) for w in (ffn2_w_gate, ffn2_w_up, ffn2_w_down)]
    w_in, w_out, w_qkv, w_o = bf(ab_w_in), bf(ab_w_out), bf(attn_w_qkv), bf(attn_w_o)

    for l in range(depth):
        i = l // 2
        xf = _ffn(xf, _row(ffn1_norm[l]), *(_layer(w, l) for w in ffn1))
        if l % 2 == 0:
            lora = rwkv_w_decay_up.shape[1]
            zeros = jnp.zeros((LANES - lora, a_width), F32)
            wdec = bf(jnp.concatenate([rwkv_w_decay_up[i], zeros], axis=0))
            waaa = bf(jnp.concatenate([zeros, rwkv_w_aaa_up[i]], axis=0))
            bsp = jnp.repeat(sg_b_spatial[i].T, b_width // n_groups, axis=1)
            r, lw, k, v, a, b, gate, bonus, yb = _mix_in(
                xf, _row(mix_norm[l]), _layer(w_in, i), _row(ab_mu_shift[i]), _row(rwkv_w0[i]), wdec,
                _row(rwkv_a0[i]), waaa, bf(rwkv_w_gate_up[i]), _row(rwkv_k_k[i]), _row(rwkv_k_a[i]),
                _row(rwkv_r_k[i]), _row(sg_ln_gain[i]), _row(sg_ln_bias[i]), sg_w_spatial[i], bsp, ones_bd,
                bsz=bsz)
            y = _scan(r, lw, k, v, a, b, bsz=bsz)
            xf = _mix_out(xf, y, bonus, gate, yb, _row(rwkv_gn_gain[i]), _row(rwkv_gn_bias[i]), ones_bd,
                          _layer(w_out, i, rows=a_width, row_block=0), _layer(w_out, i, rows=b_width, row_block=1))
        else:
            xf = _attn(xf, cos, sin, _row(mix_norm[l]), _layer(w_qkv, i), _row(attn_b_qkv[i]),
                       _row(jnp.tile(attn_q_norm[i], pairs)), _row(jnp.tile(attn_k_norm[i], pairs)),
                       attn_sinks[i], _layer(w_o, i), _row(attn_b_o[i]), ones_bd, bsz=bsz)
        xf = _ffn(xf, _row(ffn2_norm[l]), *(_layer(w, l) for w in ffn2))
    return xf.reshape(bsz, t_len, d)
```

```python
import functools
import math
from typing import NamedTuple

import jax
import jax.numpy as jnp
from jax import lax
from jax.experimental import pallas as pl
from jax.experimental.pallas import tpu as pltpu

F32 = jnp.float32
BF16 = jnp.bfloat16

HEAD_DIM = 64
RMS_EPS = 1e-6
GN_EPS = 64e-5
LN_EPS = 1e-5
ROPE_THETA = 10000.0
LOG2_E = math.log2(math.e)
WINDOW = 128
SCAN_CHUNK = 64
LANES = 128
VMEM_LIMIT = 56 * 1024 * 1024


def _cparams(*sem):
    return pltpu.CompilerParams(dimension_semantics=sem, vmem_limit_bytes=VMEM_LIMIT)


def _resident(shape):
    nd = len(shape)
    return pl.BlockSpec(shape, lambda *_: (0,) * nd, pipeline_mode=pl.Buffered(1))


class _LayerSlice(NamedTuple):
    array: jax.Array
    shape: tuple
    index: tuple


def _layer(array, layer, rows=None, row_block=0):
    shape = tuple(array.shape[1:])
    if rows is not None:
        shape = (rows,) + shape[1:]
    return _LayerSlice(array, shape, (layer, row_block, 0))


def _const_spec(c):
    if isinstance(c, _LayerSlice):
        return pl.BlockSpec((None,) + c.shape, lambda *_: c.index, pipeline_mode=pl.Buffered(1))
    return _resident(c.shape)


def _const_arg(c):
    return c.array if isinstance(c, _LayerSlice) else c


def _dot(a, b):
    return jnp.dot(a, b, preferred_element_type=F32)


def _dot_nt(a, b):
    return lax.dot_general(a, b, (((1,), (1,)), ((), ())), preferred_element_type=F32)


def _dot_tn(a, b):
    return lax.dot_general(a, b, (((0,), (0,)), ((), ())), preferred_element_type=F32)


def _rms(x, gain):
    return x * lax.rsqrt(jnp.mean(x * x, axis=-1, keepdims=True) + RMS_EPS) * gain


def _group_sum(x, ones_bd):
    xb = x.astype(BF16)
    bd = ones_bd.shape[0]
    slabs = [_dot(xb[:, c:c + bd], ones_bd) for c in range(0, x.shape[1], bd)]
    return slabs[0] if len(slabs) == 1 else jnp.concatenate(slabs, axis=1)


def _swiglu_half_step(x, gain_ref, wg_ref, wu_ref, wd_ref, n_chunks):
    h = _rms(x, gain_ref[...]).astype(BF16)
    fc = wg_ref.shape[1] // n_chunks
    acc = jnp.zeros_like(x)
    for c in range(n_chunks):
        g = _dot(h, wg_ref[:, c * fc:(c + 1) * fc])
        u = _dot(h, wu_ref[:, c * fc:(c + 1) * fc])
        act = (g * jax.nn.sigmoid(g) * u).astype(BF16)
        acc = acc + _dot(act, wd_ref[c * fc:(c + 1) * fc, :])
    return x + 0.5 * acc


def _ffn_body(x_ref, gain_ref, wg_ref, wu_ref, wd_ref, o_ref, *, n_chunks):
    o_ref[...] = _swiglu_half_step(x_ref[...], gain_ref, wg_ref, wu_ref, wd_ref, n_chunks)


def _mix_out_ffn_body(x_ref, y_ref, bonus_ref, gate_ref, yb_ref, gng_ref, gnb_ref, ones_ref, wa_ref, wb_ref,
                      gain_ref, wg_ref, wu_ref, wd_ref, o_ref, *, n_chunks):
    ones_bd = ones_ref[...]
    y = y_ref[...]
    mu = _group_sum(y, ones_bd) * (1.0 / HEAD_DIM)
    yc = y - mu
    var = _group_sum(yc * yc, ones_bd) * (1.0 / HEAD_DIM)
    yn = yc * lax.rsqrt(var + GN_EPS) * gng_ref[...] + gnb_ref[...]
    ya = ((yn + bonus_ref[...]) * gate_ref[...]).astype(BF16)
    x = x_ref[...] + _dot(ya, wa_ref[...]) + _dot(yb_ref[...], wb_ref[...])
    o_ref[...] = _swiglu_half_step(x, gain_ref, wg_ref, wu_ref, wd_ref, n_chunks)


def _ffn(x, gain, wg, wu, wd, *, tm=1024, n_chunks=11):
    n, d = x.shape
    tm = min(tm, n)
    return pl.pallas_call(
        functools.partial(_ffn_body, n_chunks=n_chunks),
        grid=(n // tm,),
        in_specs=[pl.BlockSpec((tm, d), lambda i: (i, 0))] + [_const_spec(c) for c in (gain, wg, wu, wd)],
        out_specs=pl.BlockSpec((tm, d), lambda i: (i, 0)),
        out_shape=jax.ShapeDtypeStruct((n, d), F32),
        compiler_params=_cparams("parallel"),
        name="ffn",
    )(x, *map(_const_arg, (gain, wg, wu, wd)))


def _mix_out_ffn(x, y, bonus, gate, yb, gng, gnb, ones_bd, wa, wb, gain, wg, wu, wd, *, tm=512, n_chunks=11):
    n, d = x.shape
    tm = min(tm, n)
    w = y.shape[1]
    xs = pl.BlockSpec((tm, d), lambda i: (i, 0))
    ws = pl.BlockSpec((tm, w), lambda i: (i, 0))
    consts = (gng, gnb, ones_bd, wa, wb, gain, wg, wu, wd)
    return pl.pallas_call(
        functools.partial(_mix_out_ffn_body, n_chunks=n_chunks),
        grid=(n // tm,),
        in_specs=[xs, ws, ws, ws, ws] + [_const_spec(c) for c in consts],
        out_specs=xs,
        out_shape=jax.ShapeDtypeStruct((n, d), F32),
        compiler_params=_cparams("parallel"),
        name="mix_out_ffn",
    )(x, y, bonus, gate, yb, *map(_const_arg, consts))


def _rope_body(pos_ref, freq_ref, cos_ref, sin_ref):
    ang = pos_ref[...].astype(F32) * freq_ref[...]
    lane = lax.broadcasted_iota(jnp.int32, ang.shape, 1)
    first_half = (lane % HEAD_DIM) < HEAD_DIM // 2
    cos_ref[...] = jnp.cos(ang)
    s = jnp.sin(ang)
    sin_ref[...] = jnp.where(first_half, -s, s)


def _rope_tables(pos, *, tm=2048):
    n = pos.shape[0]
    tm = min(tm, n)
    half = HEAD_DIM // 2
    inv_freq = ROPE_THETA ** (-jnp.arange(0, HEAD_DIM, 2, dtype=F32) / HEAD_DIM)
    freq = jnp.tile(inv_freq, LANES // half)[None, :]
    return pl.pallas_call(
        _rope_body,
        grid=(n // tm,),
        in_specs=[pl.BlockSpec((tm, 1), lambda i: (i, 0)), _resident((1, LANES))],
        out_specs=[pl.BlockSpec((tm, LANES), lambda i: (i, 0))] * 2,
        out_shape=[jax.ShapeDtypeStruct((n, LANES), F32)] * 2,
        compiler_params=_cparams("parallel"),
        name="rope_tables",
    )(pos, freq)


def _attn_body(sink_ref, x_ref, cos_ref, sin_ref, gain_ref, wqkv_ref, bqkv_ref, qg_ref, kg_ref, wo_ref, bo_ref,
               ones_ref, o_ref, k2_prev, v2_prev, *, n_q, n_kv, wave, slab):
    t = pl.program_id(1)
    tq = x_ref.shape[0]
    nblk = tq // WINDOW
    qw = n_q * HEAD_DIM
    kw = n_kv * HEAD_DIM
    per_kv = n_q // n_kv

    @pl.when(t == 0)
    def _():
        k2_prev[...] = jnp.zeros_like(k2_prev)
        v2_prev[...] = jnp.zeros_like(v2_prev)

    x = x_ref[...]
    h = _rms(x, gain_ref[...]).astype(BF16)
    qkv = _dot(h, wqkv_ref[...]) + bqkv_ref[...]

    lane = lax.broadcasted_iota(jnp.int32, (tq, LANES), 1)
    lo = lane < HEAD_DIM
    first_half = (lane % HEAD_DIM) < HEAD_DIM // 2
    cos = cos_ref[...]
    sin = sin_ref[...]

    ones_bd = ones_ref[...]

    def norm_rope(z, gain_pair, extra):
        inv = lax.rsqrt(_group_sum(z * z, ones_bd) * (1.0 / HEAD_DIM) + RMS_EPS) * extra
        out = []
        for p in range(z.shape[1] // LANES):
            pc = slice(p * LANES, (p + 1) * LANES)
            zg = z[:, pc] * gain_pair
            partner = jnp.where(first_half, pltpu.roll(zg, LANES - HEAD_DIM // 2, axis=1),
                                pltpu.roll(zg, HEAD_DIM // 2, axis=1))
            out.append((zg * cos + partner * sin) * inv[:, pc])
        return out

    scale = HEAD_DIM ** -0.5 * LOG2_E
    q_pairs = [qp.astype(BF16) for qp in norm_rope(qkv[:, :qw], qg_ref[...], scale)]
    k_pairs = norm_rope(qkv[:, qw:qw + kw], kg_ref[...], 1.0)
    v_pairs = [qkv[:, qw + kw + p * LANES: qw + kw + (p + 1) * LANES] for p in range(kw // LANES)]

    k2, v_top, v_bot = [], [], []
    for g in range(n_kv):
        kp = k_pairs[g // 2]
        vp = v_pairs[g // 2]
        kr = pltpu.roll(kp, HEAD_DIM, axis=1)
        vr = pltpu.roll(vp, HEAD_DIM, axis=1)
        if g % 2 == 0:
            k2.append(jnp.where(lo, kp, kr).astype(BF16))
            v_top.append(jnp.where(lo, vp, 0.0).astype(BF16))
            v_bot.append(jnp.where(lo, 0.0, vr).astype(BF16))
        else:
            k2.append(jnp.where(lo, kr, kp).astype(BF16))
            v_top.append(jnp.where(lo, vr, 0.0).astype(BF16))
            v_bot.append(jnp.where(lo, 0.0, vp).astype(BF16))

    qi = lax.broadcasted_iota(jnp.int32, (WINDOW, 2 * WINDOW), 0)
    kj = lax.broadcasted_iota(jnp.int32, (WINDOW, 2 * WINDOW), 1)
    dist = qi + WINDOW - kj
    in_window = (dist >= 0) & (dist < WINDOW)
    lo_blk = lax.broadcasted_iota(jnp.int32, (WINDOW, LANES), 1) < HEAD_DIM
    oc_row = lax.broadcasted_iota(jnp.int32, (4 * WINDOW, LANES), 0)
    oc_lane = lax.broadcasted_iota(jnp.int32, (4 * WINDOW, LANES), 1)
    ones_cols = ((oc_row < 2 * WINDOW) == (oc_lane < HEAD_DIM)).astype(BF16)

    n_pairs = per_kv // 2
    inst = [(blk, g, pp) for blk in range(nblk) for g in range(n_kv) for pp in range(n_pairs)]
    rows_of = lambda blk: slice(blk * WINDOW, (blk + 1) * WINDOW)
    valid_first = in_window & (kj >= jnp.where(t > 0, 0, WINDOW))
    bias = {0: jnp.where(valid_first, 0.0, -jnp.inf)}
    if nblk > 1:
        bias[1] = jnp.where(in_window, 0.0, -jnp.inf)
    lo_slab = lax.broadcasted_iota(jnp.int32, (slab, LANES), 1) < HEAD_DIM
    kband, vband = {}, {}
    for blk in range(nblk):
        for g in range(n_kv):
            if blk == 0:
                kprev, vtp, vbp = k2_prev[g], v2_prev[g, 0], v2_prev[g, 1]
            else:
                prows = rows_of(blk - 1)
                kprev, vtp, vbp = k2[g][prows], v_top[g][prows], v_bot[g][prows]
            rows = rows_of(blk)
            kband[blk, g] = jnp.concatenate([kprev, k2[g][rows]], axis=0)
            vals = jnp.concatenate([vtp, v_top[g][rows], vbp, v_bot[g][rows]], axis=0)
            vband[blk, g] = jnp.concatenate([vals, ones_cols], axis=1)
    o_pair = {}
    for w0 in range(0, len(inst), wave):
        wave_inst = inst[w0:w0 + wave]
        scores = {}
        for blk, g, pp in wave_inst:
            qp = q_pairs[g * n_pairs + pp][rows_of(blk)]
            zero = jnp.zeros_like(qp)
            lhs = jnp.concatenate([jnp.where(lo_blk, qp, zero), jnp.where(lo_blk, zero, qp)], axis=0)
            scores[blk, g, pp] = _dot_nt(lhs, kband[blk, g])
        p_cat, sink_term = {}, {}
        for blk, g, pp in wave_inst:
            pair = g * n_pairs + pp
            sc = scores[blk, g, pp]
            bias_blk = bias[min(blk, 1)]
            p_head, e_head = [], []
            for hd in range(2):
                sink = sink_ref[2 * pair + hd] * LOG2_E
                p_slabs, e_slabs = [], []
                for r0 in range(0, WINDOW, slab):
                    s = sc[hd * WINDOW + r0: hd * WINDOW + r0 + slab] + bias_blk[r0:r0 + slab]
                    m = jnp.maximum(jnp.max(s, axis=-1, keepdims=True), sink)
                    p_slabs.append(jnp.exp2(s - m).astype(BF16))
                    e_slabs.append(jnp.exp2(sink - m))
                p_head.append(jnp.concatenate(p_slabs, axis=0))
                e_head.append(e_slabs)
            p_cat[blk, g, pp] = jnp.concatenate(p_head, axis=1)
            sink_term[blk, g, pp] = jnp.concatenate(
                [jnp.where(lo_slab, e0, e1) for e0, e1 in zip(*e_head)], axis=0)
        for i in wave_inst:
            pv = _dot(p_cat[i], vband[i[0], i[1]])
            o_pair[i] = pv[:, :LANES] / (pv[:, LANES:] + sink_term[i])
    o_all = jnp.concatenate(
        [jnp.concatenate([o_pair[blk, g, pp] for g in range(n_kv) for pp in range(n_pairs)], axis=1).astype(BF16)
         for blk in range(nblk)], axis=0)

    last = slice((nblk - 1) * WINDOW, nblk * WINDOW)
    for g in range(n_kv):
        k2_prev[g] = k2[g][last]
        v2_prev[g, 0] = v_top[g][last]
        v2_prev[g, 1] = v_bot[g][last]

    o_ref[...] = x + _dot(o_all, wo_ref[...]) + bo_ref[...]


def _attn(x, cos, sin, gain, wqkv, bqkv, qg, kg, sinks, wo, bo, ones_bd, *, bsz, tq=512, wave=8, slab=32):
    n, d = x.shape
    t_len = n // bsz
    tq = min(tq, t_len)
    nt = t_len // tq
    n_q = wo.shape[0] // HEAD_DIM
    n_kv = (wqkv.shape[1] - wo.shape[0]) // (2 * HEAD_DIM)
    row = lambda b, t, *_: (b * nt + t, 0)
    consts = (gain, wqkv, bqkv, qg, kg, wo, bo, ones_bd)
    grid_spec = pltpu.PrefetchScalarGridSpec(
        num_scalar_prefetch=1,
        grid=(bsz, nt),
        in_specs=[pl.BlockSpec((tq, d), row), pl.BlockSpec((tq, LANES), row), pl.BlockSpec((tq, LANES), row)]
        + [_const_spec(c) for c in consts],
        out_specs=pl.BlockSpec((tq, d), row),
        scratch_shapes=[pltpu.VMEM((n_kv, WINDOW, LANES), BF16), pltpu.VMEM((n_kv, 2, WINDOW, LANES), BF16)],
    )
    return pl.pallas_call(
        functools.partial(_attn_body, n_q=n_q, n_kv=n_kv, wave=wave, slab=slab),
        grid_spec=grid_spec,
        out_shape=jax.ShapeDtypeStruct((n, d), F32),
        compiler_params=_cparams("parallel", "arbitrary"),
        name="swa_attention",
    )(sinks, x, cos, sin, *map(_const_arg, consts))


def _mix_in_body(x_ref, xp_ref, gain_ref, win_ref, mu_ref, w0_ref, wdec_ref, a0_ref, waaa_ref, wgate_ref,
                 kk_ref, ka_ref, rk_ref, lng_ref, lnb_ref, wsp_ref, bsp_ref, ones_ref,
                 r_o, lw_o, k_o, v_o, a_o, b_o, gate_o, bonus_o, yb_o, *, a_width, b_width):
    t = pl.program_id(1)
    tm = x_ref.shape[0]
    a_cols = mu_ref.shape[1]
    gain = gain_ref[...]
    h = _rms(x_ref[...], gain).astype(BF16)
    hp = _rms(xp_ref[...], gain).astype(BF16)
    z = _dot(h, win_ref[...])
    zp = _dot(hp, win_ref[:, :a_cols])
    za = z[:, :a_cols]
    prev_row = jnp.where(t > 0, zp[xp_ref.shape[0] - 1:, :], 0.0)
    row = lax.broadcasted_iota(jnp.int32, (tm, 1), 0)
    shifted = jnp.where(row == 0, prev_row, pltpu.roll(za, 1, axis=0))
    za = za + (shifted - za) * mu_ref[...]

    aw = a_width
    r = za[:, :aw]
    k = za[:, aw:2 * aw]
    v = za[:, 2 * aw:3 * aw]
    xwa = za[:, 3 * aw:3 * aw + LANES]
    xg = za[:, 3 * aw + LANES:]
    y_dec = w0_ref[...] + _dot(jnp.tanh(xwa).astype(BF16), wdec_ref[...])
    lw = (-math.exp(-0.5)) * jax.nn.sigmoid(y_dec)
    iclr = jax.nn.sigmoid(a0_ref[...] + _dot(xwa.astype(BF16), waaa_ref[...]))
    gate = _dot(jax.nn.sigmoid(xg).astype(BF16), wgate_ref[...])
    ones_bd = ones_ref[...]
    kk = k * kk_ref[...]
    kk = kk * lax.rsqrt(jnp.maximum(_group_sum(kk * kk, ones_bd), 1e-24))
    k = k * (1.0 + (iclr - 1.0) * ka_ref[...])
    r_o[...] = r
    lw_o[...] = lw
    k_o[...] = k
    v_o[...] = v
    a_o[...] = -kk
    b_o[...] = kk * iclr
    gate_o[...] = gate
    bonus_o[...] = _group_sum(r * k * rk_ref[...], ones_bd) * v

    zb = z[:, a_cols:]
    zb = 0.5 * zb * (1.0 + lax.erf(zb * math.sqrt(0.5)))
    u = zb[:, :b_width]
    s = zb[:, b_width:]
    mean = jnp.mean(s, axis=-1, keepdims=True)
    sc = s - mean
    var = jnp.mean(sc * sc, axis=-1, keepdims=True)
    s = (sc * lax.rsqrt(var + LN_EPS) * lng_ref[...] + lnb_ref[...]).astype(BF16)
    ti = lax.broadcasted_iota(jnp.int32, (WINDOW, WINDOW), 0)
    si = lax.broadcasted_iota(jnp.int32, (WINDOW, WINDOW), 1)
    n_groups = wsp_ref.shape[0]
    gd = b_width // n_groups
    for g in range(n_groups):
        w_g = jnp.where(si <= ti, wsp_ref[g], 0.0).astype(BF16)
        for c in range(tm // WINDOW):
            rows = slice(c * WINDOW, (c + 1) * WINDOW)
            cols = slice(g * gd, (g + 1) * gd)
            mixed = _dot(w_g, s[rows, cols]) + bsp_ref[:, cols]
            yb_o[rows, cols] = (u[rows, cols] * mixed).astype(yb_o.dtype)


def _mix_in(x, gain, win, mu, w0, wdec, a0, waaa, wgate, kk, ka, rk, lng, lnb, wsp, bsp, ones_bd, *, bsz, tm=512):
    n, d = x.shape
    t_len = n // bsz
    nt = t_len // tm
    a_width = w0.shape[1]
    b_width = lng.shape[1]
    prev_rows = 8
    row = lambda b, t: (b * nt + t, 0)
    prev = lambda b, t: (jnp.maximum((b * nt + t) * (tm // prev_rows) - 1, 0), 0)
    consts = [gain, win, mu, w0, wdec, a0, waaa, wgate, kk, ka, rk, lng, lnb, wsp, bsp, ones_bd]
    out_spec = pl.BlockSpec((tm, a_width), row)
    return pl.pallas_call(
        functools.partial(_mix_in_body, a_width=a_width, b_width=b_width),
        grid=(bsz, nt),
        in_specs=[pl.BlockSpec((tm, d), row), pl.BlockSpec((prev_rows, d), prev)] + [_const_spec(c) for c in consts],
        out_specs=[out_spec] * 9,
        out_shape=[jax.ShapeDtypeStruct((n, a_width), F32)] * 8 + [jax.ShapeDtypeStruct((n, b_width), BF16)],
        compiler_params=_cparams("parallel", "parallel"),
        name="mix_in",
    )(x, x, *map(_const_arg, consts))


def _scan_body(r_ref, lw_ref, k_ref, v_ref, a_ref, b_ref, y_ref, state, *, n_heads):
    c_len = SCAN_CHUNK
    assert c_len == HEAD_DIM and 2 * HEAD_DIM == LANES

    @pl.when(pl.program_id(1) == 0)
    def _():
        state[...] = jnp.zeros_like(state)

    two_c = 2 * c_len
    ti = lax.broadcasted_iota(jnp.int32, (c_len, c_len), 0)
    si = lax.broadcasted_iota(jnp.int32, (c_len, c_len), 1)
    tril = (si <= ti).astype(F32)
    mid = c_len // 2 - 1
    row2 = lax.broadcasted_iota(jnp.int32, (two_c, two_c), 0)
    col2 = lax.broadcasted_iota(jnp.int32, (two_c, two_c), 1)
    t_idx, s_idx = row2 % c_len, col2 % c_len
    tri = (s_idx < t_idx) | ((row2 >= c_len) & (s_idx == t_idx))
    anti_eye = (t_idx == s_idx).astype(F32)
    lo2 = col2 < HEAD_DIM
    same_head = (row2 < HEAD_DIM) == lo2
    lo1 = lax.broadcasted_iota(jnp.int32, (c_len, LANES), 1) < HEAD_DIM

    n_seq, ts, _ = r_ref.shape
    n_chunks = ts // c_len
    n_pairs = n_heads * HEAD_DIM // LANES
    pcols = [slice(p * LANES, (p + 1) * LANES) for p in range(n_pairs)]

    pre = {}
    for c in range(n_chunks):
        rows = slice(c * c_len, (c + 1) * c_len)
        for sq in range(n_seq):
            lw = lw_ref[sq, rows, :]
            cum = jnp.dot(tril, lw, preferred_element_type=F32, precision=lax.Precision.HIGHEST)
            l_mid = cum[mid:mid + 1, :]
            l_end = cum[c_len - 1:, :]
            e_neg = jnp.exp(l_mid - cum)
            e_pos = jnp.exp(cum - l_mid)
            e_end = jnp.exp(l_end - cum)
            pre[c, sq] = dict(
                rows=rows,
                p_mid=jnp.exp(l_mid), p_end=jnp.exp(l_end),
                a_t=a_ref[sq, rows, :] * jnp.exp(cum - lw - l_mid),
                r_t=r_ref[sq, rows, :] * e_pos,
                b_t=b_ref[sq, rows, :] * e_neg,
                k_t=k_ref[sq, rows, :] * e_neg,
                b_e=b_ref[sq, rows, :] * e_end,
                k_e=k_ref[sq, rows, :] * e_end,
                v=v_ref[sq, rows, :])

    groups = [(sq, p) for sq in range(n_seq) for p in range(n_pairs)]
    inst = [(c, sq, p) for c in range(n_chunks) for sq, p in groups]
    lhs_s, qkm, v_b, bk = {}, {}, {}, {}
    for c, sq, p in inst:
        pc, pr = pcols[p], pre[c, sq]
        lhs = jnp.concatenate([pr["a_t"][:, pc], pr["r_t"][:, pc]], axis=0)
        lhs_s[c, sq, p] = (lhs * pr["p_mid"][:, pc]).astype(BF16)
        lhs_b = lhs.astype(BF16)
        bt = pr["b_t"][:, pc].astype(BF16)
        kt = pr["k_t"][:, pc].astype(BF16)
        qk0 = _dot_nt(jnp.where(lo2, lhs_b, 0), jnp.concatenate([kt, bt], axis=0))
        qk1 = _dot_nt(jnp.where(lo2, 0, lhs_b), jnp.concatenate([bt, kt], axis=0))
        qkm[c, sq, p] = (jnp.where(tri, qk0, 0.0), jnp.where(tri, qk1, 0.0))
        v_b[c, sq, p] = pr["v"][:, pc].astype(BF16)
        bk[c, sq, p] = jnp.concatenate([pr["b_e"][:, pc], pr["k_e"][:, pc]], axis=0).astype(BF16)
    a_rb, av, work = {}, {}, {}
    for i in inst:
        q0, q1 = qkm[i]
        a_rb[i] = jnp.where(lo1, q1[c_len:], q0[c_len:]).astype(BF16)
        v2 = jnp.concatenate([jnp.where(lo1, v_b[i], 0), jnp.where(lo1, 0, v_b[i])], axis=0)
        av[i] = _dot(jnp.where(lo2, q0, q1).astype(BF16), v2)
        work[i] = jnp.where(same_head, jnp.concatenate([q1[:c_len], q0[:c_len]], axis=0), anti_eye)
    for _ in range(c_len.bit_length() - 1):
        for i in inst:
            wb = work[i].astype(BF16)
            prod = _dot(jnp.where(same_head, wb, 0), wb)
            work[i] = jnp.where(same_head, prod, work[i] + prod)
    inv = {i: jnp.where(same_head, 0.0, work[i]).astype(BF16) for i in inst}

    s_cur = {g: state[gi] for gi, g in enumerate(groups)}
    for c in range(n_chunks):
        sa = {g: _dot_nt(lhs_s[(c,) + g], s_cur[g].astype(BF16)) for g in groups}
        u2 = {}
        for g in groups:
            x = (sa[g][:c_len] + av[(c,) + g][:c_len]).astype(BF16)
            rhs = jnp.concatenate([jnp.where(lo1, x, 0), jnp.where(lo1, 0, x)], axis=0)
            u2[g] = _dot(inv[(c,) + g], rhs)
        for g in groups:
            i = (c,) + g
            uv = jnp.concatenate([(u2[g][:c_len] + u2[g][c_len:]).astype(BF16), v_b[i]], axis=0)
            grown = jnp.where(same_head, _dot_tn(uv, bk[i]), 0.0)
            s_cur[g] = s_cur[g] * pre[c, g[0]]["p_end"][:, pcols[g[1]]] + grown
        for g in groups:
            i = (c,) + g
            y = sa[g][c_len:] + av[i][c_len:] + _dot(a_rb[i], u2[g].astype(BF16))
            y_ref[g[0], pre[c, g[0]]["rows"], pcols[g[1]]] = y
    for gi, g in enumerate(groups):
        state[gi] = s_cur[g]


def _scan(r, lw, k, v, a, b, *, bsz, ts=256, n_seq=4):
    n, width = r.shape
    t_len = n // bsz
    ts = min(ts, t_len)
    n_seq = math.gcd(n_seq, bsz)
    n_heads = width // HEAD_DIM
    spec = pl.BlockSpec((n_seq, ts, width), lambda bb, t: (bb, t, 0))
    y = pl.pallas_call(
        functools.partial(_scan_body, n_heads=n_heads),
        grid=(bsz // n_seq, t_len // ts),
        in_specs=[spec] * 6,
        out_specs=spec,
        out_shape=jax.ShapeDtypeStruct((bsz, t_len, width), F32),
        scratch_shapes=[pltpu.VMEM((n_seq * width // LANES, LANES, LANES), F32)],
        compiler_params=_cparams("parallel", "arbitrary"),
        name="rwkv_scan",
    )(*(z.reshape(bsz, t_len, width) for z in (r, lw, k, v, a, b)))
    return y.reshape(n, width)


def _row(v):
    return v.reshape(1, -1)


def kernel(x, positions, ffn1_norm, ffn1_w_gate, ffn1_w_up, ffn1_w_down, mix_norm, ffn2_norm, ffn2_w_gate, ffn2_w_up, ffn2_w_down, ab_w_in, ab_mu_shift, rwkv_w0, rwkv_w_decay_up, rwkv_a0, rwkv_w_aaa_up, rwkv_w_gate_up, rwkv_k_k, rwkv_k_a, rwkv_r_k, rwkv_gn_gain, rwkv_gn_bias, sg_ln_gain, sg_ln_bias, sg_w_spatial, sg_b_spatial, ab_w_out, attn_w_qkv, attn_b_qkv, attn_q_norm, attn_k_norm, attn_sinks, attn_w_o, attn_b_o):
    bsz, t_len, d = x.shape
    depth = ffn1_norm.shape[0]
    a_width = rwkv_w0.shape[1]
    b_width = sg_ln_gain.shape[1]
    n_groups = sg_w_spatial.shape[1]
    assert a_width == b_width
    bf = lambda w: w.astype(BF16)

    xf = x.reshape(bsz * t_len, d)
    cos, sin = _rope_tables(positions.reshape(-1, 1))
    head_id = jnp.arange(2 * LANES) // HEAD_DIM
    ones_bd = (head_id[:, None] == head_id[None, :]).astype(BF16)
    pairs = LANES // HEAD_DIM
    ffn1 = [bf(w) for w in (ffn1_w_gate, ffn1_w_up, ffn1_w_down)]
    ffn2 = [bf(w) for w in (ffn2_w_gate, ffn2_w_up, ffn2_w_down)]
    w_in, w_out, w_qkv, w_o = bf(ab_w_in), bf(ab_w_out), bf(attn_w_qkv), bf(attn_w_o)

    for l in range(depth):
        i = l // 2
        xf = _ffn(xf, _row(ffn1_norm[l]), *(_layer(w, l) for w in ffn1))
        if l % 2 == 0:
            lora = rwkv_w_decay_up.shape[1]
            zeros = jnp.zeros((LANES - lora, a_width), F32)
            wdec = bf(jnp.concatenate([rwkv_w_decay_up[i], zeros], axis=0))
            waaa = bf(jnp.concatenate([zeros, rwkv_w_aaa_up[i]], axis=0))
            bsp = jnp.repeat(sg_b_spatial[i].T, b_width // n_groups, axis=1)
            r, lw, k, v, a, b, gate, bonus, yb = _mix_in(
                xf, _row(mix_norm[l]), _layer(w_in, i), _row(ab_mu_shift[i]), _row(rwkv_w0[i]), wdec,
                _row(rwkv_a0[i]), waaa, bf(rwkv_w_gate_up[i]), _row(rwkv_k_k[i]), _row(rwkv_k_a[i]),
                _row(rwkv_r_k[i]), _row(sg_ln_gain[i]), _row(sg_ln_bias[i]), sg_w_spatial[i], bsp, ones_bd,
                bsz=bsz)
            y = _scan(r, lw, k, v, a, b, bsz=bsz)
            xf = _mix_out_ffn(xf, y, bonus, gate, yb, _row(rwkv_gn_gain[i]), _row(rwkv_gn_bias[i]), ones_bd,
                              _layer(w_out, i, rows=a_width, row_block=0), _layer(w_out, i, rows=b_width, row_block=1),
                              _row(ffn2_norm[l]), *(_layer(w, l) for w in ffn2))
        else:
            xf = _attn(xf, cos, sin, _row(mix_norm[l]), _layer(w_qkv, i), _row(attn_b_qkv[i]),
                       _row(jnp.tile(attn_q_norm[i], pairs)), _row(jnp.tile(attn_k_norm[i], pairs)),
                       attn_sinks[i], _layer(w_o, i), _row(attn_b_o[i]), ones_bd, bsz=bsz)
            xf = _ffn(xf, _row(ffn2_norm[l]), *(_layer(w, l) for w in ffn2))
    return xf.reshape(bsz, t_len, d)
```

```python
import functools
import math
from typing import NamedTuple

import jax
import jax.numpy as jnp
from jax import lax
from jax.experimental import pallas as pl
from jax.experimental.pallas import tpu as pltpu

F32 = jnp.float32
BF16 = jnp.bfloat16

HEAD_DIM = 64
RMS_EPS = 1e-6
GN_EPS = 64e-5
LN_EPS = 1e-5
ROPE_THETA = 10000.0
LOG2_E = math.log2(math.e)
WINDOW = 128
SCAN_CHUNK = 64
LANES = 128
VMEM_LIMIT = 56 * 1024 * 1024


def _cparams(*sem):
    return pltpu.CompilerParams(dimension_semantics=sem, vmem_limit_bytes=VMEM_LIMIT)


def _resident(shape):
    nd = len(shape)
    return pl.BlockSpec(shape, lambda *_: (0,) * nd, pipeline_mode=pl.Buffered(1))


class _LayerSlice(NamedTuple):
    array: jax.Array
    shape: tuple
    index: tuple


def _layer(array, layer, rows=None, row_block=0):
    shape = tuple(array.shape[1:])
    if rows is not None:
        shape = (rows,) + shape[1:]
    return _LayerSlice(array, shape, (layer, row_block, 0))


def _const_spec(c):
    if isinstance(c, _LayerSlice):
        return pl.BlockSpec((None,) + c.shape, lambda *_: c.index, pipeline_mode=pl.Buffered(1))
    return _resident(c.shape)


def _const_arg(c):
    return c.array if isinstance(c, _LayerSlice) else c


def _dot(a, b):
    return jnp.dot(a, b, preferred_element_type=F32)


def _dot_nt(a, b):
    return lax.dot_general(a, b, (((1,), (1,)), ((), ())), preferred_element_type=F32)


def _dot_tn(a, b):
    return lax.dot_general(a, b, (((0,), (0,)), ((), ())), preferred_element_type=F32)


def _rms(x, gain):
    return x * lax.rsqrt(jnp.mean(x * x, axis=-1, keepdims=True) + RMS_EPS) * gain


def _group_sum(x, ones_bd):
    xb = x.astype(BF16)
    bd = ones_bd.shape[0]
    slabs = [_dot(xb[:, c:c + bd], ones_bd) for c in range(0, x.shape[1], bd)]
    return slabs[0] if len(slabs) == 1 else jnp.concatenate(slabs, axis=1)


def _swiglu_half_step(x, gain_ref, wg_ref, wu_ref, wd_ref, n_chunks):
    h = _rms(x, gain_ref[...]).astype(BF16)
    fc = wg_ref.shape[1] // n_chunks
    def gate_up(c):
        return _dot(h, wg_ref[:, c * fc:(c + 1) * fc]), _dot(h, wu_ref[:, c * fc:(c + 1) * fc])

    acc = jnp.zeros_like(x)
    g, u = gate_up(0)
    for c in range(n_chunks):
        nxt = gate_up(c + 1) if c + 1 < n_chunks else None
        act = (g * jax.nn.sigmoid(g) * u).astype(BF16)
        acc = acc + _dot(act, wd_ref[c * fc:(c + 1) * fc, :])
        if nxt is not None:
            g, u = nxt
    return x + 0.5 * acc


def _ffn_body(x_ref, gain_ref, wg_ref, wu_ref, wd_ref, o_ref, *, n_chunks):
    o_ref[...] = _swiglu_half_step(x_ref[...], gain_ref, wg_ref, wu_ref, wd_ref, n_chunks)


def _mix_out_ffn_body(x_ref, y_ref, bonus_ref, gate_ref, yb_ref, gng_ref, gnb_ref, ones_ref, wa_ref, wb_ref,
                      gain_ref, wg_ref, wu_ref, wd_ref, o_ref, *, n_chunks):
    ones_bd = ones_ref[...]
    y = y_ref[...]
    mu = _group_sum(y, ones_bd) * (1.0 / HEAD_DIM)
    yc = y - mu
    var = _group_sum(yc * yc, ones_bd) * (1.0 / HEAD_DIM)
    yn = yc * lax.rsqrt(var + GN_EPS) * gng_ref[...] + gnb_ref[...]
    ya = ((yn + bonus_ref[...]) * gate_ref[...]).astype(BF16)
    x = x_ref[...] + _dot(ya, wa_ref[...]) + _dot(yb_ref[...], wb_ref[...])
    o_ref[...] = _swiglu_half_step(x, gain_ref, wg_ref, wu_ref, wd_ref, n_chunks)


def _ffn(x, gain, wg, wu, wd, *, tm=1024, n_chunks=11):
    n, d = x.shape
    tm = min(tm, n)
    return pl.pallas_call(
        functools.partial(_ffn_body, n_chunks=n_chunks),
        grid=(n // tm,),
        in_specs=[pl.BlockSpec((tm, d), lambda i: (i, 0))] + [_const_spec(c) for c in (gain, wg, wu, wd)],
        out_specs=pl.BlockSpec((tm, d), lambda i: (i, 0)),
        out_shape=jax.ShapeDtypeStruct((n, d), F32),
        compiler_params=_cparams("parallel"),
        name="ffn",
    )(x, *map(_const_arg, (gain, wg, wu, wd)))


def _mix_out_ffn(x, y, bonus, gate, yb, gng, gnb, ones_bd, wa, wb, gain, wg, wu, wd, *, tm=512, n_chunks=11):
    n, d = x.shape
    tm = min(tm, n)
    w = y.shape[1]
    xs = pl.BlockSpec((tm, d), lambda i: (i, 0))
    ws = pl.BlockSpec((tm, w), lambda i: (i, 0))
    consts = (gng, gnb, ones_bd, wa, wb, gain, wg, wu, wd)
    return pl.pallas_call(
        functools.partial(_mix_out_ffn_body, n_chunks=n_chunks),
        grid=(n // tm,),
        in_specs=[xs, ws, ws, ws, ws] + [_const_spec(c) for c in consts],
        out_specs=xs,
        out_shape=jax.ShapeDtypeStruct((n, d), F32),
        compiler_params=_cparams("parallel"),
        name="mix_out_ffn",
    )(x, y, bonus, gate, yb, *map(_const_arg, consts))


def _rope_body(pos_ref, freq_ref, cos_ref, sin_ref):
    ang = pos_ref[...].astype(F32) * freq_ref[...]
    lane = lax.broadcasted_iota(jnp.int32, ang.shape, 1)
    first_half = (lane % HEAD_DIM) < HEAD_DIM // 2
    cos_ref[...] = jnp.cos(ang)
    s = jnp.sin(ang)
    sin_ref[...] = jnp.where(first_half, -s, s)


def _rope_tables(pos, *, tm=2048):
    n = pos.shape[0]
    tm = min(tm, n)
    half = HEAD_DIM // 2
    inv_freq = ROPE_THETA ** (-jnp.arange(0, HEAD_DIM, 2, dtype=F32) / HEAD_DIM)
    freq = jnp.tile(inv_freq, LANES // half)[None, :]
    return pl.pallas_call(
        _rope_body,
        grid=(n // tm,),
        in_specs=[pl.BlockSpec((tm, 1), lambda i: (i, 0)), _resident((1, LANES))],
        out_specs=[pl.BlockSpec((tm, LANES), lambda i: (i, 0))] * 2,
        out_shape=[jax.ShapeDtypeStruct((n, LANES), F32)] * 2,
        compiler_params=_cparams("parallel"),
        name="rope_tables",
    )(pos, freq)


def _attn_body(sink_ref, x_ref, cos_ref, sin_ref, gain_ref, wqkv_ref, bqkv_ref, qg_ref, kg_ref, wo_ref, bo_ref,
               ones_ref, o_ref, k2_prev, v2_prev, *, n_q, n_kv, slab):
    t = pl.program_id(1)
    tq = x_ref.shape[0]
    nblk = tq // WINDOW
    qw = n_q * HEAD_DIM
    kw = n_kv * HEAD_DIM
    per_kv = n_q // n_kv

    @pl.when(t == 0)
    def _():
        k2_prev[...] = jnp.zeros_like(k2_prev)
        v2_prev[...] = jnp.zeros_like(v2_prev)

    h = _rms(x_ref[...], gain_ref[...]).astype(BF16)

    def proj(c0, c1):
        return _dot(h, wqkv_ref[:, c0:c1]) + bqkv_ref[:, c0:c1]

    lane = lax.broadcasted_iota(jnp.int32, (tq, LANES), 1)
    lo = lane < HEAD_DIM
    first_half = (lane % HEAD_DIM) < HEAD_DIM // 2
    cos = cos_ref[...]
    sin = sin_ref[...]

    ones_bd = ones_ref[...]

    def norm_rope(z, gain_pair, extra):
        inv = lax.rsqrt(_group_sum(z * z, ones_bd) * (1.0 / HEAD_DIM) + RMS_EPS) * extra
        out = []
        for p in range(z.shape[1] // LANES):
            pc = slice(p * LANES, (p + 1) * LANES)
            zg = z[:, pc] * gain_pair
            partner = jnp.where(first_half, pltpu.roll(zg, LANES - HEAD_DIM // 2, axis=1),
                                pltpu.roll(zg, HEAD_DIM // 2, axis=1))
            out.append((zg * cos + partner * sin) * inv[:, pc])
        return out

    scale = HEAD_DIM ** -0.5 * LOG2_E
    q_group = ones_bd.shape[0]
    kv = proj(qw, qw + 2 * kw)
    q_raw = [proj(0, q_group)]
    k_pairs = norm_rope(kv[:, :kw], kg_ref[...], 1.0)
    v_pairs = [kv[:, kw + p * LANES: kw + (p + 1) * LANES] for p in range(kw // LANES)]
    q_pairs = []
    for c in range(q_group, qw + q_group, q_group):
        if c < qw:
            q_raw.append(proj(c, c + q_group))
        q_pairs += [qp.astype(BF16) for qp in norm_rope(q_raw[c // q_group - 1], qg_ref[...], scale)]

    k2, v_top, v_bot = [], [], []
    for g in range(n_kv):
        kp = k_pairs[g // 2]
        vp = v_pairs[g // 2]
        kr = pltpu.roll(kp, HEAD_DIM, axis=1)
        vr = pltpu.roll(vp, HEAD_DIM, axis=1)
        if g % 2 == 0:
            k2.append(jnp.where(lo, kp, kr).astype(BF16))
            v_top.append(jnp.where(lo, vp, 0.0).astype(BF16))
            v_bot.append(jnp.where(lo, 0.0, vr).astype(BF16))
        else:
            k2.append(jnp.where(lo, kr, kp).astype(BF16))
            v_top.append(jnp.where(lo, vr, 0.0).astype(BF16))
            v_bot.append(jnp.where(lo, 0.0, vp).astype(BF16))

    qi = lax.broadcasted_iota(jnp.int32, (WINDOW, 2 * WINDOW), 0)
    kj = lax.broadcasted_iota(jnp.int32, (WINDOW, 2 * WINDOW), 1)
    dist = qi + WINDOW - kj
    in_window = (dist >= 0) & (dist < WINDOW)
    lo_blk = lax.broadcasted_iota(jnp.int32, (WINDOW, LANES), 1) < HEAD_DIM
    oc_row = lax.broadcasted_iota(jnp.int32, (4 * WINDOW, LANES), 0)
    oc_lane = lax.broadcasted_iota(jnp.int32, (4 * WINDOW, LANES), 1)
    ones_cols = ((oc_row < 2 * WINDOW) == (oc_lane < HEAD_DIM)).astype(BF16)

    n_pairs = per_kv // 2
    inst = [(blk, g, pp) for blk in range(nblk) for g in range(n_kv) for pp in range(n_pairs)]
    rows_of = lambda blk: slice(blk * WINDOW, (blk + 1) * WINDOW)
    valid_first = in_window & (kj >= jnp.where(t > 0, 0, WINDOW))
    bias = {0: jnp.where(valid_first, 0.0, -jnp.inf)}
    if nblk > 1:
        bias[1] = jnp.where(in_window, 0.0, -jnp.inf)
    lo_slab = lax.broadcasted_iota(jnp.int32, (slab, LANES), 1) < HEAD_DIM
    kband, vband = {}, {}
    for blk in range(nblk):
        for g in range(n_kv):
            if blk == 0:
                kprev, vtp, vbp = k2_prev[g], v2_prev[g, 0], v2_prev[g, 1]
            else:
                prows = rows_of(blk - 1)
                kprev, vtp, vbp = k2[g][prows], v_top[g][prows], v_bot[g][prows]
            rows = rows_of(blk)
            kband[blk, g] = jnp.concatenate([kprev, k2[g][rows]], axis=0)
            vals = jnp.concatenate([vtp, v_top[g][rows], vbp, v_bot[g][rows]], axis=0)
            vband[blk, g] = jnp.concatenate([vals, ones_cols], axis=1)
    for blk_now in range(nblk):
        wave_inst = [i for i in inst if i[0] == blk_now]
        o_pair = {}
        scores = {}
        for blk, g, pp in wave_inst:
            qp = q_pairs[g * n_pairs + pp][rows_of(blk)]
            zero = jnp.zeros_like(qp)
            lhs = jnp.concatenate([jnp.where(lo_blk, qp, zero), jnp.where(lo_blk, zero, qp)], axis=0)
            scores[blk, g, pp] = _dot_nt(lhs, kband[blk, g])
        p_cat, sink_term = {}, {}
        for blk, g, pp in wave_inst:
            pair = g * n_pairs + pp
            sc = scores[blk, g, pp]
            bias_blk = bias[min(blk, 1)]
            p_head, e_head = [], []
            for hd in range(2):
                sink = sink_ref[2 * pair + hd] * LOG2_E
                p_slabs, e_slabs = [], []
                for r0 in range(0, WINDOW, slab):
                    s = sc[hd * WINDOW + r0: hd * WINDOW + r0 + slab] + bias_blk[r0:r0 + slab]
                    m = jnp.maximum(jnp.max(s, axis=-1, keepdims=True), sink)
                    p_slabs.append(jnp.exp2(s - m).astype(BF16))
                    e_slabs.append(jnp.exp2(sink - m))
                p_head.append(jnp.concatenate(p_slabs, axis=0))
                e_head.append(e_slabs)
            p_cat[blk, g, pp] = jnp.concatenate(p_head, axis=1)
            sink_term[blk, g, pp] = jnp.concatenate(
                [jnp.where(lo_slab, e0, e1) for e0, e1 in zip(*e_head)], axis=0)
        for i in wave_inst:
            pv = _dot(p_cat[i], vband[i[0], i[1]])
            o_pair[i] = pv[:, :LANES] / (pv[:, LANES:] + sink_term[i])
        o_blk = jnp.concatenate([o_pair[i] for i in wave_inst], axis=1).astype(BF16)
        rows = rows_of(blk_now)
        o_ref[rows, :] = x_ref[rows, :] + _dot(o_blk, wo_ref[...]) + bo_ref[...]

    last = rows_of(nblk - 1)
    for g in range(n_kv):
        k2_prev[g] = k2[g][last]
        v2_prev[g, 0] = v_top[g][last]
        v2_prev[g, 1] = v_bot[g][last]


def _attn(x, cos, sin, gain, wqkv, bqkv, qg, kg, sinks, wo, bo, ones_bd, *, bsz, tq=512, slab=32):
    n, d = x.shape
    t_len = n // bsz
    tq = min(tq, t_len)
    nt = t_len // tq
    n_q = wo.shape[0] // HEAD_DIM
    n_kv = (wqkv.shape[1] - wo.shape[0]) // (2 * HEAD_DIM)
    row = lambda b, t, *_: (b * nt + t, 0)
    consts = (gain, wqkv, bqkv, qg, kg, wo, bo, ones_bd)
    grid_spec = pltpu.PrefetchScalarGridSpec(
        num_scalar_prefetch=1,
        grid=(bsz, nt),
        in_specs=[pl.BlockSpec((tq, d), row), pl.BlockSpec((tq, LANES), row), pl.BlockSpec((tq, LANES), row)]
        + [_const_spec(c) for c in consts],
        out_specs=pl.BlockSpec((tq, d), row),
        scratch_shapes=[pltpu.VMEM((n_kv, WINDOW, LANES), BF16), pltpu.VMEM((n_kv, 2, WINDOW, LANES), BF16)],
    )
    return pl.pallas_call(
        functools.partial(_attn_body, n_q=n_q, n_kv=n_kv, slab=slab),
        grid_spec=grid_spec,
        out_shape=jax.ShapeDtypeStruct((n, d), F32),
        compiler_params=_cparams("parallel", "arbitrary"),
        name="swa_attention",
    )(sinks, x, cos, sin, *map(_const_arg, consts))


def _mix_in_body(x_ref, xp_ref, gain_ref, win_ref, mu_ref, w0_ref, wdec_ref, a0_ref, waaa_ref, wgate_ref,
                 kk_ref, ka_ref, rk_ref, lng_ref, lnb_ref, wsp_ref, bsp_ref, ones_ref,
                 r_o, lw_o, k_o, v_o, a_o, b_o, gate_o, bonus_o, yb_o, *, a_width, b_width):
    t = pl.program_id(1)
    tm = x_ref.shape[0]
    a_cols = mu_ref.shape[1]
    gain = gain_ref[...]
    h = _rms(x_ref[...], gain).astype(BF16)
    n_prev = xp_ref.shape[0]
    h_ext = jnp.concatenate([_rms(xp_ref[...], gain).astype(BF16), h], axis=0)
    first_of_seq = lax.broadcasted_iota(jnp.int32, (tm, 1), 0) == jnp.where(t == 0, 0, -1)

    def proj(c0, c1):
        return _dot(h, win_ref[:, c0:c1])

    def proj_ext(c0, c1):
        return _dot(h_ext, win_ref[:, c0:c1]), (c0, c1)

    def shift_mix(projected):
        z_ext, (c0, c1) = projected
        z = z_ext[n_prev:]
        shifted = jnp.where(first_of_seq, 0.0, pltpu.roll(z_ext, 1, axis=0)[n_prev:])
        return z + (shifted - z) * mu_ref[:, c0:c1]

    def gelu(z):
        return 0.5 * z * (1.0 + lax.erf(z * math.sqrt(0.5)))

    aw = a_width
    ones_bd = ones_ref[...]
    p_lora = proj_ext(3 * aw, a_cols)
    p_k = proj_ext(aw, 2 * aw)
    lora = shift_mix(p_lora)
    xwa = lora[:, :LANES]
    xg = lora[:, LANES:]
    y_dec = w0_ref[...] + _dot(jnp.tanh(xwa).astype(BF16), wdec_ref[...])
    lw_o[...] = (-math.exp(-0.5)) * jax.nn.sigmoid(y_dec)
    iclr = jax.nn.sigmoid(a0_ref[...] + _dot(xwa.astype(BF16), waaa_ref[...]))
    gate_o[...] = _dot(jax.nn.sigmoid(xg).astype(BF16), wgate_ref[...])
    p_r = proj_ext(0, aw)
    k = shift_mix(p_k)
    kk = k * kk_ref[...]
    kk = kk * lax.rsqrt(jnp.maximum(_group_sum(kk * kk, ones_bd), 1e-24))
    k = k * (1.0 + (iclr - 1.0) * ka_ref[...])
    k_o[...] = k
    a_o[...] = -kk
    b_o[...] = kk * iclr
    p_v = proj_ext(2 * aw, 3 * aw)
    r = shift_mix(p_r)
    r_o[...] = r
    rk = _group_sum(r * k * rk_ref[...], ones_bd)
    z_s = proj(a_cols + b_width, a_cols + 2 * b_width)
    v = shift_mix(p_v)
    v_o[...] = v
    bonus_o[...] = rk * v

    z_u = proj(a_cols, a_cols + b_width)
    s = gelu(z_s)
    u = gelu(z_u)
    mean = jnp.mean(s, axis=-1, keepdims=True)
    sc = s - mean
    var = jnp.mean(sc * sc, axis=-1, keepdims=True)
    s = (sc * lax.rsqrt(var + LN_EPS) * lng_ref[...] + lnb_ref[...]).astype(BF16)
    ti = lax.broadcasted_iota(jnp.int32, (WINDOW, WINDOW), 0)
    si = lax.broadcasted_iota(jnp.int32, (WINDOW, WINDOW), 1)
    n_groups = wsp_ref.shape[0]
    gd = b_width // n_groups
    for g in range(n_groups):
        w_g = jnp.where(si <= ti, wsp_ref[g], 0.0).astype(BF16)
        for c in range(tm // WINDOW):
            rows = slice(c * WINDOW, (c + 1) * WINDOW)
            cols = slice(g * gd, (g + 1) * gd)
            mixed = _dot(w_g, s[rows, cols]) + bsp_ref[:, cols]
            yb_o[rows, cols] = (u[rows, cols] * mixed).astype(yb_o.dtype)


def _mix_in(x, gain, win, mu, w0, wdec, a0, waaa, wgate, kk, ka, rk, lng, lnb, wsp, bsp, ones_bd, *, bsz, tm=512):
    n, d = x.shape
    t_len = n // bsz
    nt = t_len // tm
    a_width = w0.shape[1]
    b_width = lng.shape[1]
    prev_rows = 16
    row = lambda b, t: (b * nt + t, 0)
    prev = lambda b, t: (jnp.maximum((b * nt + t) * (tm // prev_rows) - 1, 0), 0)
    consts = [gain, win, mu, w0, wdec, a0, waaa, wgate, kk, ka, rk, lng, lnb, wsp, bsp, ones_bd]
    out_spec = pl.BlockSpec((tm, a_width), row)
    return pl.pallas_call(
        functools.partial(_mix_in_body, a_width=a_width, b_width=b_width),
        grid=(bsz, nt),
        in_specs=[pl.BlockSpec((tm, d), row), pl.BlockSpec((prev_rows, d), prev)] + [_const_spec(c) for c in consts],
        out_specs=[out_spec] * 9,
        out_shape=[jax.ShapeDtypeStruct((n, a_width), F32)] * 8 + [jax.ShapeDtypeStruct((n, b_width), BF16)],
        compiler_params=_cparams("parallel", "parallel"),
        name="mix_in",
    )(x, x, *map(_const_arg, consts))


def _scan_body(r_ref, lw_ref, k_ref, v_ref, a_ref, b_ref, y_ref, state, *, n_heads):
    c_len = SCAN_CHUNK
    assert c_len == HEAD_DIM and 2 * HEAD_DIM == LANES

    @pl.when(pl.program_id(1) == 0)
    def _():
        state[...] = jnp.zeros_like(state)

    two_c = 2 * c_len
    ti = lax.broadcasted_iota(jnp.int32, (c_len, c_len), 0)
    si = lax.broadcasted_iota(jnp.int32, (c_len, c_len), 1)
    tril = (si <= ti).astype(F32)
    mid = c_len // 2 - 1
    row2 = lax.broadcasted_iota(jnp.int32, (two_c, two_c), 0)
    col2 = lax.broadcasted_iota(jnp.int32, (two_c, two_c), 1)
    t_idx, s_idx = row2 % c_len, col2 % c_len
    tri = (s_idx < t_idx) | ((row2 >= c_len) & (s_idx == t_idx))
    anti_eye = (t_idx == s_idx).astype(F32)
    lo2 = col2 < HEAD_DIM
    same_head = (row2 < HEAD_DIM) == lo2
    lo1 = lax.broadcasted_iota(jnp.int32, (c_len, LANES), 1) < HEAD_DIM

    n_seq, ts, _ = r_ref.shape
    n_chunks = ts // c_len
    n_pairs = n_heads * HEAD_DIM // LANES
    pcols = [slice(p * LANES, (p + 1) * LANES) for p in range(n_pairs)]

    pre = {}
    for c in range(n_chunks):
        rows = slice(c * c_len, (c + 1) * c_len)
        for sq in range(n_seq):
            lw = lw_ref[sq, rows, :]
            cum = jnp.dot(tril, lw, preferred_element_type=F32, precision=lax.Precision.HIGHEST)
            l_mid = cum[mid:mid + 1, :]
            l_end = cum[c_len - 1:, :]
            e_neg = jnp.exp(l_mid - cum)
            e_pos = jnp.exp(cum - l_mid)
            e_end = jnp.exp(l_end - cum)
            pre[c, sq] = dict(
                rows=rows,
                p_mid=jnp.exp(l_mid), p_end=jnp.exp(l_end),
                a_t=a_ref[sq, rows, :] * jnp.exp(cum - lw - l_mid),
                r_t=r_ref[sq, rows, :] * e_pos,
                b_t=b_ref[sq, rows, :] * e_neg,
                k_t=k_ref[sq, rows, :] * e_neg,
                b_e=b_ref[sq, rows, :] * e_end,
                k_e=k_ref[sq, rows, :] * e_end,
                v=v_ref[sq, rows, :])

    groups = [(sq, p) for sq in range(n_seq) for p in range(n_pairs)]
    inst = [(c, sq, p) for c in range(n_chunks) for sq, p in groups]
    lhs_s, qkm, v_b, bk = {}, {}, {}, {}
    for c, sq, p in inst:
        pc, pr = pcols[p], pre[c, sq]
        lhs = jnp.concatenate([pr["a_t"][:, pc], pr["r_t"][:, pc]], axis=0)
        lhs_s[c, sq, p] = (lhs * pr["p_mid"][:, pc]).astype(BF16)
        lhs_b = lhs.astype(BF16)
        bt = pr["b_t"][:, pc].astype(BF16)
        kt = pr["k_t"][:, pc].astype(BF16)
        qk0 = _dot_nt(jnp.where(lo2, lhs_b, 0), jnp.concatenate([kt, bt], axis=0))
        qk1 = _dot_nt(jnp.where(lo2, 0, lhs_b), jnp.concatenate([bt, kt], axis=0))
        qkm[c, sq, p] = (jnp.where(tri, qk0, 0.0), jnp.where(tri, qk1, 0.0))
        v_b[c, sq, p] = pr["v"][:, pc].astype(BF16)
        bk[c, sq, p] = jnp.concatenate([pr["b_e"][:, pc], pr["k_e"][:, pc]], axis=0).astype(BF16)
    a_rb, av, work = {}, {}, {}
    for i in inst:
        q0, q1 = qkm[i]
        a_rb[i] = jnp.where(lo1, q1[c_len:], q0[c_len:]).astype(BF16)
        v2 = jnp.concatenate([jnp.where(lo1, v_b[i], 0), jnp.where(lo1, 0, v_b[i])], axis=0)
        av[i] = _dot(jnp.where(lo2, q0, q1).astype(BF16), v2)
        work[i] = jnp.where(same_head, jnp.concatenate([q1[:c_len], q0[:c_len]], axis=0), anti_eye)
    for _ in range(c_len.bit_length() - 1):
        for i in inst:
            wb = work[i].astype(BF16)
            prod = _dot(jnp.where(same_head, wb, 0), wb)
            work[i] = jnp.where(same_head, prod, work[i] + prod)
    inv = {i: jnp.where(same_head, 0.0, work[i]).astype(BF16) for i in inst}

    s_cur = {g: state[gi] for gi, g in enumerate(groups)}
    for c in range(n_chunks):
        sa = {g: _dot_nt(lhs_s[(c,) + g], s_cur[g].astype(BF16)) for g in groups}
        u2 = {}
        for g in groups:
            x = (sa[g][:c_len] + av[(c,) + g][:c_len]).astype(BF16)
            rhs = jnp.concatenate([jnp.where(lo1, x, 0), jnp.where(lo1, 0, x)], axis=0)
            u2[g] = _dot(inv[(c,) + g], rhs)
        for g in groups:
            i = (c,) + g
            uv = jnp.concatenate([(u2[g][:c_len] + u2[g][c_len:]).astype(BF16), v_b[i]], axis=0)
            grown = jnp.where(same_head, _dot_tn(uv, bk[i]), 0.0)
            s_cur[g] = s_cur[g] * pre[c, g[0]]["p_end"][:, pcols[g[1]]] + grown
        for g in groups:
            i = (c,) + g
            y = sa[g][c_len:] + av[i][c_len:] + _dot(a_rb[i], u2[g].astype(BF16))
            y_ref[g[0], pre[c, g[0]]["rows"], pcols[g[1]]] = y
    for gi, g in enumerate(groups):
        state[gi] = s_cur[g]


def _scan(r, lw, k, v, a, b, *, bsz, ts=256, n_seq=4):
    n, width = r.shape
    t_len = n // bsz
    ts = min(ts, t_len)
    n_seq = math.gcd(n_seq, bsz)
    n_heads = width // HEAD_DIM
    spec = pl.BlockSpec((n_seq, ts, width), lambda bb, t: (bb, t, 0))
    y = pl.pallas_call(
        functools.partial(_scan_body, n_heads=n_heads),
        grid=(bsz // n_seq, t_len // ts),
        in_specs=[spec] * 6,
        out_specs=spec,
        out_shape=jax.ShapeDtypeStruct((bsz, t_len, width), F32),
        scratch_shapes=[pltpu.VMEM((n_seq * width // LANES, LANES, LANES), F32)],
        compiler_params=_cparams("parallel", "arbitrary"),
        name="rwkv_scan",
    )(*(z.reshape(bsz, t_len, width) for z in (r, lw, k, v, a, b)))
    return y.reshape(n, width)


def _row(v):
    return v.reshape(1, -1)


def kernel(x, positions, ffn1_norm, ffn1_w_gate, ffn1_w_up, ffn1_w_down, mix_norm, ffn2_norm, ffn2_w_gate, ffn2_w_up, ffn2_w_down, ab_w_in, ab_mu_shift, rwkv_w0, rwkv_w_decay_up, rwkv_a0, rwkv_w_aaa_up, rwkv_w_gate_up, rwkv_k_k, rwkv_k_a, rwkv_r_k, rwkv_gn_gain, rwkv_gn_bias, sg_ln_gain, sg_ln_bias, sg_w_spatial, sg_b_spatial, ab_w_out, attn_w_qkv, attn_b_qkv, attn_q_norm, attn_k_norm, attn_sinks, attn_w_o, attn_b_o):
    bsz, t_len, d = x.shape
    depth = ffn1_norm.shape[0]
    a_width = rwkv_w0.shape[1]
    b_width = sg_ln_gain.shape[1]
    n_groups = sg_w_spatial.shape[1]
    assert a_width == b_width
    bf = lambda w: w.astype(BF16)

    xf = x.reshape(bsz * t_len, d)
    cos, sin = _rope_tables(positions.reshape(-1, 1))
    head_id = jnp.arange(2 * LANES) // HEAD_DIM
    ones_bd = (head_id[:, None] == head_id[None, :]).astype(BF16)
    pairs = LANES // HEAD_DIM
    ffn1 = [bf(w) for w in (ffn1_w_gate, ffn1_w_up, ffn1_w_down)]
    ffn2 = [bf(w) for w in (ffn2_w_gate, ffn2_w_up, ffn2_w_down)]
    w_in, w_out, w_qkv, w_o = bf(ab_w_in), bf(ab_w_out), bf(attn_w_qkv), bf(attn_w_o)

    for l in range(depth):
        i = l // 2
        xf = _ffn(xf, _row(ffn1_norm[l]), *(_layer(w, l) for w in ffn1))
        if l % 2 == 0:
            lora = rwkv_w_decay_up.shape[1]
            zeros = jnp.zeros((LANES - lora, a_width), F32)
            wdec = bf(jnp.concatenate([rwkv_w_decay_up[i], zeros], axis=0))
            waaa = bf(jnp.concatenate([zeros, rwkv_w_aaa_up[i]], axis=0))
            bsp = jnp.repeat(sg_b_spatial[i].T, b_width // n_groups, axis=1)
            r, lw, k, v, a, b, gate, bonus, yb = _mix_in(
                xf, _row(mix_norm[l]), _layer(w_in, i), _row(ab_mu_shift[i]), _row(rwkv_w0[i]), wdec,
                _row(rwkv_a0[i]), waaa, bf(rwkv_w_gate_up[i]), _row(rwkv_k_k[i]), _row(rwkv_k_a[i]),
                _row(rwkv_r_k[i]), _row(sg_ln_gain[i]), _row(sg_ln_bias[i]), sg_w_spatial[i], bsp, ones_bd,
                bsz=bsz)
            y = _scan(r, lw, k, v, a, b, bsz=bsz)
            xf = _mix_out_ffn(xf, y, bonus, gate, yb, _row(rwkv_gn_gain[i]), _row(rwkv_gn_bias[i]), ones_bd,
                              _layer(w_out, i, rows=a_width, row_block=0), _layer(w_out, i, rows=b_width, row_block=1),
                              _row(ffn2_norm[l]), *(_layer(w, l) for w in ffn2))
        else:
            xf = _attn(xf, cos, sin, _row(mix_norm[l]), _layer(w_qkv, i), _row(attn_b_qkv[i]),
                       _row(jnp.tile(attn_q_norm[i], pairs)), _row(jnp.tile(attn_k_norm[i], pairs)),
                       attn_sinks[i], _layer(w_o, i), _row(attn_b_o[i]), ones_bd, bsz=bsz)
            xf = _ffn(xf, _row(ffn2_norm[l]), *(_layer(w, l) for w in ffn2))
    return xf.reshape(bsz, t_len, d)
```

```python
import functools
import math
from typing import NamedTuple

import jax
import jax.numpy as jnp
from jax import lax
from jax.experimental import pallas as pl
from jax.experimental.pallas import tpu as pltpu

F32 = jnp.float32
BF16 = jnp.bfloat16

HEAD_DIM = 64
RMS_EPS = 1e-6
GN_EPS = 64e-5
LN_EPS = 1e-5
ROPE_THETA = 10000.0
LOG2_E = math.log2(math.e)
WINDOW = 128
SCAN_CHUNK = 64
LANES = 128
VMEM_LIMIT = 56 * 1024 * 1024


def _cparams(*sem):
    return pltpu.CompilerParams(dimension_semantics=sem, vmem_limit_bytes=VMEM_LIMIT)


def _resident(shape):
    nd = len(shape)
    return pl.BlockSpec(shape, lambda *_: (0,) * nd, pipeline_mode=pl.Buffered(1))


class _LayerSlice(NamedTuple):
    array: jax.Array
    shape: tuple
    index: tuple


def _layer(array, layer, rows=None, row_block=0):
    shape = tuple(array.shape[1:])
    if rows is not None:
        shape = (rows,) + shape[1:]
    return _LayerSlice(array, shape, (layer, row_block, 0))


def _const_spec(c):
    if isinstance(c, _LayerSlice):
        return pl.BlockSpec((None,) + c.shape, lambda *_: c.index, pipeline_mode=pl.Buffered(1))
    return _resident(c.shape)


def _const_arg(c):
    return c.array if isinstance(c, _LayerSlice) else c


def _dot(a, b):
    return jnp.dot(a, b, preferred_element_type=F32)


def _dot_nt(a, b):
    return lax.dot_general(a, b, (((1,), (1,)), ((), ())), preferred_element_type=F32)


def _dot_tn(a, b):
    return lax.dot_general(a, b, (((0,), (0,)), ((), ())), preferred_element_type=F32)


def _rms(x, gain):
    return x * lax.rsqrt(jnp.mean(x * x, axis=-1, keepdims=True) + RMS_EPS) * gain


def _group_sum(x, ones_bd):
    xb = x.astype(BF16)
    bd = ones_bd.shape[0]
    slabs = [_dot(xb[:, c:c + bd], ones_bd) for c in range(0, x.shape[1], bd)]
    return slabs[0] if len(slabs) == 1 else jnp.concatenate(slabs, axis=1)


def _swiglu_half_step(x, gain_ref, wg_ref, wu_ref, wd_ref, n_chunks):
    h = _rms(x, gain_ref[...]).astype(BF16)
    fc = wg_ref.shape[1] // n_chunks

    def gate_up(c):
        return _dot(h, wg_ref[:, c * fc:(c + 1) * fc]), _dot(h, wu_ref[:, c * fc:(c + 1) * fc])

    acc = jnp.zeros_like(x)
    g, u = gate_up(0)
    for c in range(n_chunks):
        nxt = gate_up(c + 1) if c + 1 < n_chunks else None
        act = (g * jax.nn.sigmoid(g) * u).astype(BF16)
        acc = acc + _dot(act, wd_ref[c * fc:(c + 1) * fc, :])
        if nxt is not None:
            g, u = nxt
    return x + 0.5 * acc


def _ffn_body(x_ref, gain_ref, wg_ref, wu_ref, wd_ref, o_ref, *, n_chunks):
    o_ref[...] = _swiglu_half_step(x_ref[...], gain_ref, wg_ref, wu_ref, wd_ref, n_chunks)


def _mix_out_ffn_body(x_ref, y_ref, bonus_ref, gate_ref, yb_ref, gng_ref, gnb_ref, ones_ref, wa_ref, wb_ref,
                      gain_ref, wg_ref, wu_ref, wd_ref, o_ref, *, n_chunks):
    ones_bd = ones_ref[...]
    y = y_ref[...]
    mu = _group_sum(y, ones_bd) * (1.0 / HEAD_DIM)
    yc = y - mu
    var = _group_sum(yc * yc, ones_bd) * (1.0 / HEAD_DIM)
    yn = yc * lax.rsqrt(var + GN_EPS) * gng_ref[...] + gnb_ref[...]
    ya = ((yn + bonus_ref[...]) * gate_ref[...]).astype(BF16)
    x = x_ref[...] + _dot(ya, wa_ref[...]) + _dot(yb_ref[...], wb_ref[...])
    o_ref[...] = _swiglu_half_step(x, gain_ref, wg_ref, wu_ref, wd_ref, n_chunks)


def _ffn(x, gain, wg, wu, wd, *, tm=1024, n_chunks=11):
    n, d = x.shape
    tm = min(tm, n)
    return pl.pallas_call(
        functools.partial(_ffn_body, n_chunks=n_chunks),
        grid=(n // tm,),
        in_specs=[pl.BlockSpec((tm, d), lambda i: (i, 0))] + [_const_spec(c) for c in (gain, wg, wu, wd)],
        out_specs=pl.BlockSpec((tm, d), lambda i: (i, 0)),
        out_shape=jax.ShapeDtypeStruct((n, d), F32),
        compiler_params=_cparams("parallel"),
        name="ffn",
    )(x, *map(_const_arg, (gain, wg, wu, wd)))


def _mix_out_ffn(x, y, bonus, gate, yb, gng, gnb, ones_bd, wa, wb, gain, wg, wu, wd, *, tm=512, n_chunks=11):
    n, d = x.shape
    tm = min(tm, n)
    w = y.shape[1]
    xs = pl.BlockSpec((tm, d), lambda i: (i, 0))
    ws = pl.BlockSpec((tm, w), lambda i: (i, 0))
    consts = (gng, gnb, ones_bd, wa, wb, gain, wg, wu, wd)
    return pl.pallas_call(
        functools.partial(_mix_out_ffn_body, n_chunks=n_chunks),
        grid=(n // tm,),
        in_specs=[xs, ws, ws, ws, ws] + [_const_spec(c) for c in consts],
        out_specs=xs,
        out_shape=jax.ShapeDtypeStruct((n, d), F32),
        compiler_params=_cparams("parallel"),
        name="mix_out_ffn",
    )(x, y, bonus, gate, yb, *map(_const_arg, consts))


def _rope_body(pos_ref, freq_ref, cos_ref, sin_ref):
    ang = pos_ref[...].astype(F32) * freq_ref[...]
    lane = lax.broadcasted_iota(jnp.int32, ang.shape, 1)
    first_half = (lane % HEAD_DIM) < HEAD_DIM // 2
    cos_ref[...] = jnp.cos(ang)
    s = jnp.sin(ang)
    sin_ref[...] = jnp.where(first_half, -s, s)


def _rope_tables(pos, *, tm=2048):
    n = pos.shape[0]
    tm = min(tm, n)
    half = HEAD_DIM // 2
    inv_freq = ROPE_THETA ** (-jnp.arange(0, HEAD_DIM, 2, dtype=F32) / HEAD_DIM)
    freq = jnp.tile(inv_freq, LANES // half)[None, :]
    return pl.pallas_call(
        _rope_body,
        grid=(n // tm,),
        in_specs=[pl.BlockSpec((tm, 1), lambda i: (i, 0)), _resident((1, LANES))],
        out_specs=[pl.BlockSpec((tm, LANES), lambda i: (i, 0))] * 2,
        out_shape=[jax.ShapeDtypeStruct((n, LANES), F32)] * 2,
        compiler_params=_cparams("parallel"),
        name="rope_tables",
    )(pos, freq)


def _attn_body(sink_ref, x_ref, cos_ref, sin_ref, gain_ref, wqkv_ref, bqkv_ref, qg_ref, kg_ref, wo_ref, bo_ref,
               ones_ref, o_ref, k2_prev, v2_prev, *, n_q, n_kv, slab):
    t = pl.program_id(1)
    tq = x_ref.shape[0]
    nblk = tq // WINDOW
    qw = n_q * HEAD_DIM
    kw = n_kv * HEAD_DIM
    per_kv = n_q // n_kv

    @pl.when(t == 0)
    def _():
        k2_prev[...] = jnp.zeros_like(k2_prev)
        v2_prev[...] = jnp.zeros_like(v2_prev)

    h = _rms(x_ref[...], gain_ref[...]).astype(BF16)

    def proj(c0, c1):
        return _dot(h, wqkv_ref[:, c0:c1]) + bqkv_ref[:, c0:c1]

    lane = lax.broadcasted_iota(jnp.int32, (tq, LANES), 1)
    lo = lane < HEAD_DIM
    first_half = (lane % HEAD_DIM) < HEAD_DIM // 2
    cos = cos_ref[...]
    sin = sin_ref[...]

    ones_bd = ones_ref[...]

    def norm_rope(z, gain_pair, extra):
        inv = lax.rsqrt(_group_sum(z * z, ones_bd) * (1.0 / HEAD_DIM) + RMS_EPS) * extra
        out = []
        for p in range(z.shape[1] // LANES):
            pc = slice(p * LANES, (p + 1) * LANES)
            zg = z[:, pc] * gain_pair
            partner = jnp.where(first_half, pltpu.roll(zg, LANES - HEAD_DIM // 2, axis=1),
                                pltpu.roll(zg, HEAD_DIM // 2, axis=1))
            out.append((zg * cos + partner * sin) * inv[:, pc])
        return out

    scale = HEAD_DIM ** -0.5 * LOG2_E
    q_group = ones_bd.shape[0]
    kv = proj(qw, qw + 2 * kw)
    q_raw = [proj(0, q_group)]
    k_pairs = norm_rope(kv[:, :kw], kg_ref[...], 1.0)
    v_pairs = [kv[:, kw + p * LANES: kw + (p + 1) * LANES] for p in range(kw // LANES)]
    q_pairs = []
    for c in range(q_group, qw + q_group, q_group):
        if c < qw:
            q_raw.append(proj(c, c + q_group))
        q_pairs += [qp.astype(BF16) for qp in norm_rope(q_raw[c // q_group - 1], qg_ref[...], scale)]

    k2, v_top, v_bot = [], [], []
    for g in range(n_kv):
        kp = k_pairs[g // 2]
        vp = v_pairs[g // 2]
        kr = pltpu.roll(kp, HEAD_DIM, axis=1)
        vr = pltpu.roll(vp, HEAD_DIM, axis=1)
        if g % 2 == 0:
            k2.append(jnp.where(lo, kp, kr).astype(BF16))
            v_top.append(jnp.where(lo, vp, 0.0).astype(BF16))
            v_bot.append(jnp.where(lo, 0.0, vr).astype(BF16))
        else:
            k2.append(jnp.where(lo, kr, kp).astype(BF16))
            v_top.append(jnp.where(lo, vr, 0.0).astype(BF16))
            v_bot.append(jnp.where(lo, 0.0, vp).astype(BF16))

    qi = lax.broadcasted_iota(jnp.int32, (WINDOW, 2 * WINDOW), 0)
    kj = lax.broadcasted_iota(jnp.int32, (WINDOW, 2 * WINDOW), 1)
    dist = qi + WINDOW - kj
    in_window = (dist >= 0) & (dist < WINDOW)
    lo_blk = lax.broadcasted_iota(jnp.int32, (WINDOW, LANES), 1) < HEAD_DIM
    oc_row = lax.broadcasted_iota(jnp.int32, (4 * WINDOW, LANES), 0)
    oc_lane = lax.broadcasted_iota(jnp.int32, (4 * WINDOW, LANES), 1)
    ones_cols = ((oc_row < 2 * WINDOW) == (oc_lane < HEAD_DIM)).astype(BF16)

    n_pairs = per_kv // 2
    inst = [(blk, g, pp) for blk in range(nblk) for g in range(n_kv) for pp in range(n_pairs)]
    rows_of = lambda blk: slice(blk * WINDOW, (blk + 1) * WINDOW)
    valid_first = in_window & (kj >= jnp.where(t > 0, 0, WINDOW))
    bias = {0: jnp.where(valid_first, 0.0, -jnp.inf)}
    if nblk > 1:
        bias[1] = jnp.where(in_window, 0.0, -jnp.inf)
    lo_slab = lax.broadcasted_iota(jnp.int32, (slab, LANES), 1) < HEAD_DIM
    kband, vband = {}, {}
    for blk in range(nblk):
        for g in range(n_kv):
            if blk == 0:
                kprev, vtp, vbp = k2_prev[g], v2_prev[g, 0], v2_prev[g, 1]
            else:
                prows = rows_of(blk - 1)
                kprev, vtp, vbp = k2[g][prows], v_top[g][prows], v_bot[g][prows]
            rows = rows_of(blk)
            kband[blk, g] = jnp.concatenate([kprev, k2[g][rows]], axis=0)
            vals = jnp.concatenate([vtp, v_top[g][rows], vbp, v_bot[g][rows]], axis=0)
            vband[blk, g] = jnp.concatenate([vals, ones_cols], axis=1)
    for blk_now in range(nblk):
        wave_inst = [i for i in inst if i[0] == blk_now]
        o_pair = {}
        scores = {}
        for blk, g, pp in wave_inst:
            qp = q_pairs[g * n_pairs + pp][rows_of(blk)]
            zero = jnp.zeros_like(qp)
            lhs = jnp.concatenate([jnp.where(lo_blk, qp, zero), jnp.where(lo_blk, zero, qp)], axis=0)
            scores[blk, g, pp] = _dot_nt(lhs, kband[blk, g])
        p_cat, sink_term = {}, {}
        for blk, g, pp in wave_inst:
            pair = g * n_pairs + pp
            sc = scores[blk, g, pp]
            bias_blk = bias[min(blk, 1)]
            p_head, e_head = [], []
            for hd in range(2):
                sink = sink_ref[2 * pair + hd] * LOG2_E
                p_slabs, e_slabs = [], []
                for r0 in range(0, WINDOW, slab):
                    s = sc[hd * WINDOW + r0: hd * WINDOW + r0 + slab] + bias_blk[r0:r0 + slab]
                    m = jnp.maximum(jnp.max(s, axis=-1, keepdims=True), sink)
                    p_slabs.append(jnp.exp2(s - m).astype(BF16))
                    e_slabs.append(jnp.exp2(sink - m))
                p_head.append(jnp.concatenate(p_slabs, axis=0))
                e_head.append(e_slabs)
            p_cat[blk, g, pp] = jnp.concatenate(p_head, axis=1)
            sink_term[blk, g, pp] = jnp.concatenate(
                [jnp.where(lo_slab, e0, e1) for e0, e1 in zip(*e_head)], axis=0)
        for i in wave_inst:
            pv = _dot(p_cat[i], vband[i[0], i[1]])
            o_pair[i] = pv[:, :LANES] / (pv[:, LANES:] + sink_term[i])
        o_blk = jnp.concatenate([o_pair[i] for i in wave_inst], axis=1).astype(BF16)
        rows = rows_of(blk_now)
        o_ref[rows, :] = x_ref[rows, :] + _dot(o_blk, wo_ref[...]) + bo_ref[...]

    last = rows_of(nblk - 1)
    for g in range(n_kv):
        k2_prev[g] = k2[g][last]
        v2_prev[g, 0] = v_top[g][last]
        v2_prev[g, 1] = v_bot[g][last]


def _attn(x, cos, sin, gain, wqkv, bqkv, qg, kg, sinks, wo, bo, ones_bd, *, bsz, tq=512, slab=32):
    n, d = x.shape
    t_len = n // bsz
    tq = min(tq, t_len)
    nt = t_len // tq
    n_q = wo.shape[0] // HEAD_DIM
    n_kv = (wqkv.shape[1] - wo.shape[0]) // (2 * HEAD_DIM)
    row = lambda b, t, *_: (b * nt + t, 0)
    consts = (gain, wqkv, bqkv, qg, kg, wo, bo, ones_bd)
    grid_spec = pltpu.PrefetchScalarGridSpec(
        num_scalar_prefetch=1,
        grid=(bsz, nt),
        in_specs=[pl.BlockSpec((tq, d), row), pl.BlockSpec((tq, LANES), row), pl.BlockSpec((tq, LANES), row)]
        + [_const_spec(c) for c in consts],
        out_specs=pl.BlockSpec((tq, d), row),
        scratch_shapes=[pltpu.VMEM((n_kv, WINDOW, LANES), BF16), pltpu.VMEM((n_kv, 2, WINDOW, LANES), BF16)],
    )
    return pl.pallas_call(
        functools.partial(_attn_body, n_q=n_q, n_kv=n_kv, slab=slab),
        grid_spec=grid_spec,
        out_shape=jax.ShapeDtypeStruct((n, d), F32),
        compiler_params=_cparams("parallel", "arbitrary"),
        name="swa_attention",
    )(sinks, x, cos, sin, *map(_const_arg, consts))


def _mix_in_body(x_ref, xp_ref, gain_ref, win_ref, mu_ref, w0_ref, wdec_ref, a0_ref, waaa_ref, wgate_ref,
                 kk_ref, ka_ref, rk_ref, lng_ref, lnb_ref, wsp_ref, bsp_ref, ones_ref,
                 r_o, lw_o, k_o, v_o, a_o, b_o, gate_o, bonus_o, yb_o, *, a_width, b_width):
    t = pl.program_id(1)
    tm = x_ref.shape[0]
    a_cols = mu_ref.shape[1]
    gain = gain_ref[...]
    h = _rms(x_ref[...], gain).astype(BF16)
    n_prev = xp_ref.shape[0]
    h_ext = jnp.concatenate([_rms(xp_ref[...], gain).astype(BF16), h], axis=0)
    first_of_seq = lax.broadcasted_iota(jnp.int32, (tm, 1), 0) == jnp.where(t == 0, 0, -1)

    def proj(c0, c1):
        return _dot(h, win_ref[:, c0:c1])

    def proj_ext(c0, c1):
        return _dot(h_ext, win_ref[:, c0:c1]), (c0, c1)

    def shift_mix(projected):
        z_ext, (c0, c1) = projected
        z = z_ext[n_prev:]
        shifted = jnp.where(first_of_seq, 0.0, pltpu.roll(z_ext, 1, axis=0)[n_prev:])
        return z + (shifted - z) * mu_ref[:, c0:c1]

    def gelu(z):
        return 0.5 * z * (1.0 + lax.erf(z * math.sqrt(0.5)))

    aw = a_width
    ones_bd = ones_ref[...]
    p_lora = proj_ext(3 * aw, a_cols)
    p_k = proj_ext(aw, 2 * aw)
    lora = shift_mix(p_lora)
    xwa = lora[:, :LANES]
    xg = lora[:, LANES:]
    y_dec = w0_ref[...] + _dot(jnp.tanh(xwa).astype(BF16), wdec_ref[...])
    lw_o[...] = (-math.exp(-0.5)) * jax.nn.sigmoid(y_dec)
    iclr = jax.nn.sigmoid(a0_ref[...] + _dot(xwa.astype(BF16), waaa_ref[...]))
    gate_o[...] = _dot(jax.nn.sigmoid(xg).astype(BF16), wgate_ref[...])
    p_r = proj_ext(0, aw)
    k = shift_mix(p_k)
    kk = k * kk_ref[...]
    kk = kk * lax.rsqrt(jnp.maximum(_group_sum(kk * kk, ones_bd), 1e-24))
    k = k * (1.0 + (iclr - 1.0) * ka_ref[...])
    k_o[...] = k
    a_o[...] = -kk
    b_o[...] = kk * iclr
    p_v = proj_ext(2 * aw, 3 * aw)
    r = shift_mix(p_r)
    r_o[...] = r
    rk = _group_sum(r * k * rk_ref[...], ones_bd)
    z_s = proj(a_cols + b_width, a_cols + 2 * b_width)
    v = shift_mix(p_v)
    v_o[...] = v
    bonus_o[...] = rk * v

    z_u = proj(a_cols, a_cols + b_width)
    s = gelu(z_s)
    u = gelu(z_u)
    mean = jnp.mean(s, axis=-1, keepdims=True)
    sc = s - mean
    var = jnp.mean(sc * sc, axis=-1, keepdims=True)
    s = (sc * lax.rsqrt(var + LN_EPS) * lng_ref[...] + lnb_ref[...]).astype(BF16)
    ti = lax.broadcasted_iota(jnp.int32, (WINDOW, WINDOW), 0)
    si = lax.broadcasted_iota(jnp.int32, (WINDOW, WINDOW), 1)
    n_groups = wsp_ref.shape[0]
    gd = b_width // n_groups
    for g in range(n_groups):
        w_g = jnp.where(si <= ti, wsp_ref[g], 0.0).astype(BF16)
        for c in range(tm // WINDOW):
            rows = slice(c * WINDOW, (c + 1) * WINDOW)
            cols = slice(g * gd, (g + 1) * gd)
            mixed = _dot(w_g, s[rows, cols]) + bsp_ref[:, cols]
            yb_o[rows, cols] = (u[rows, cols] * mixed).astype(yb_o.dtype)


def _mix_in(x, gain, win, mu, w0, wdec, a0, waaa, wgate, kk, ka, rk, lng, lnb, wsp, bsp, ones_bd, *, bsz, tm=512):
    n, d = x.shape
    t_len = n // bsz
    nt = t_len // tm
    a_width = w0.shape[1]
    b_width = lng.shape[1]
    prev_rows = 16
    row = lambda b, t: (b * nt + t, 0)
    prev = lambda b, t: (jnp.maximum((b * nt + t) * (tm // prev_rows) - 1, 0), 0)
    consts = [gain, win, mu, w0, wdec, a0, waaa, wgate, kk, ka, rk, lng, lnb, wsp, bsp, ones_bd]
    out_spec = pl.BlockSpec((tm, a_width), row)
    return pl.pallas_call(
        functools.partial(_mix_in_body, a_width=a_width, b_width=b_width),
        grid=(bsz, nt),
        in_specs=[pl.BlockSpec((tm, d), row), pl.BlockSpec((prev_rows, d), prev)] + [_const_spec(c) for c in consts],
        out_specs=[out_spec] * 9,
        out_shape=[jax.ShapeDtypeStruct((n, a_width), F32)] * 8 + [jax.ShapeDtypeStruct((n, b_width), BF16)],
        compiler_params=_cparams("parallel", "parallel"),
        name="mix_in",
    )(x, x, *map(_const_arg, consts))


def _scan_body(r_ref, lw_ref, k_ref, v_ref, a_ref, b_ref, y_ref, state, *, n_heads):
    c_len = SCAN_CHUNK
    assert c_len == HEAD_DIM and 2 * HEAD_DIM == LANES

    @pl.when(pl.program_id(1) == 0)
    def _():
        state[...] = jnp.zeros_like(state)

    two_c = 2 * c_len
    ti = lax.broadcasted_iota(jnp.int32, (c_len, c_len), 0)
    si = lax.broadcasted_iota(jnp.int32, (c_len, c_len), 1)
    tril = (si <= ti).astype(BF16)
    mid = c_len // 2 - 1
    row2 = lax.broadcasted_iota(jnp.int32, (two_c, two_c), 0)
    col2 = lax.broadcasted_iota(jnp.int32, (two_c, two_c), 1)
    t_idx, s_idx = row2 % c_len, col2 % c_len
    tri = (s_idx < t_idx) | ((row2 >= c_len) & (s_idx == t_idx))
    anti_eye = (t_idx == s_idx).astype(BF16)
    lo2 = col2 < HEAD_DIM
    same_head = (row2 < HEAD_DIM) == lo2
    lo1 = lax.broadcasted_iota(jnp.int32, (c_len, LANES), 1) < HEAD_DIM

    n_seq, ts, _ = r_ref.shape
    n_chunks = ts // c_len
    n_pairs = n_heads * HEAD_DIM // LANES
    pcols = [slice(p * LANES, (p + 1) * LANES) for p in range(n_pairs)]

    pre = {}
    for c in range(n_chunks):
        rows = slice(c * c_len, (c + 1) * c_len)
        for sq in range(n_seq):
            lw = lw_ref[sq, rows, :]
            lw_hi = lw.astype(BF16)
            lw_mid = (lw - lw_hi.astype(F32)).astype(BF16)
            lw_lo = (lw - lw_hi.astype(F32) - lw_mid.astype(F32)).astype(BF16)
            cum = _dot(tril, lw_hi) + _dot(tril, lw_mid) + _dot(tril, lw_lo)
            l_mid = cum[mid:mid + 1, :]
            l_end = cum[c_len - 1:, :]
            e_neg = jnp.exp(l_mid - cum)
            e_end = jnp.exp(l_end - cum)
            k, b = k_ref[sq, rows, :], b_ref[sq, rows, :]
            a_t = a_ref[sq, rows, :] * jnp.exp(cum - lw - l_mid)
            r_t = r_ref[sq, rows, :] * jnp.exp(cum - l_mid)
            p_mid = jnp.exp(l_mid)
            scaled = dict(
                a_t=a_t,
                r_t=r_t,
                a_s=a_t * p_mid,
                r_s=r_t * p_mid,
                b_t=b * e_neg,
                k_t=k * e_neg,
                b_e=b * e_end,
                k_e=k * e_end,
                v=v_ref[sq, rows, :])
            pre[c, sq] = dict({name: val.astype(BF16) for name, val in scaled.items()},
                              rows=rows, p_end=jnp.exp(l_end))

    groups = [(sq, p) for sq in range(n_seq) for p in range(n_pairs)]
    inst = [(c, sq, p) for c in range(n_chunks) for sq, p in groups]
    lhs_s, qkm, v_b, bk = {}, {}, {}, {}
    for c, sq, p in inst:
        pc, pr = pcols[p], pre[c, sq]
        lhs_b = jnp.concatenate([pr["a_t"][:, pc], pr["r_t"][:, pc]], axis=0)
        lhs_s[c, sq, p] = jnp.concatenate([pr["a_s"][:, pc], pr["r_s"][:, pc]], axis=0)
        bt = pr["b_t"][:, pc]
        kt = pr["k_t"][:, pc]
        qk0 = _dot_nt(jnp.where(lo2, lhs_b, 0), jnp.concatenate([kt, bt], axis=0))
        qk1 = _dot_nt(jnp.where(lo2, 0, lhs_b), jnp.concatenate([bt, kt], axis=0))
        qkm[c, sq, p] = (jnp.where(tri, qk0, 0.0).astype(BF16), jnp.where(tri, qk1, 0.0).astype(BF16))
        v_b[c, sq, p] = pr["v"][:, pc]
        bk[c, sq, p] = jnp.concatenate([pr["b_e"][:, pc], pr["k_e"][:, pc]], axis=0)
    a_rb, av, work = {}, {}, {}
    for i in inst:
        q0, q1 = qkm[i]
        a_rb[i] = jnp.where(lo1, q1[c_len:], q0[c_len:])
        v2 = jnp.concatenate([jnp.where(lo1, v_b[i], 0), jnp.where(lo1, 0, v_b[i])], axis=0)
        av[i] = _dot(jnp.where(lo2, q0, q1), v2)
        work[i] = jnp.where(same_head, jnp.concatenate([q1[:c_len], q0[:c_len]], axis=0), anti_eye)
    for _ in range(c_len.bit_length() - 1):
        for i in inst:
            prod = _dot(jnp.where(same_head, work[i], 0), work[i]).astype(BF16)
            work[i] = jnp.where(same_head, prod, work[i] + prod)
    inv = {i: jnp.where(same_head, 0, work[i]) for i in inst}

    s_cur = {g: state[gi] for gi, g in enumerate(groups)}
    for c in range(n_chunks):
        sa = {g: _dot_nt(lhs_s[(c,) + g], s_cur[g].astype(BF16)) for g in groups}
        u2 = {}
        for g in groups:
            x = (sa[g][:c_len] + av[(c,) + g][:c_len]).astype(BF16)
            rhs = jnp.concatenate([jnp.where(lo1, x, 0), jnp.where(lo1, 0, x)], axis=0)
            u2[g] = _dot(inv[(c,) + g], rhs)
        for g in groups:
            i = (c,) + g
            uv = jnp.concatenate([(u2[g][:c_len] + u2[g][c_len:]).astype(BF16), v_b[i]], axis=0)
            grown = jnp.where(same_head, _dot_tn(uv, bk[i]), 0.0)
            s_cur[g] = s_cur[g] * pre[c, g[0]]["p_end"][:, pcols[g[1]]] + grown
        for g in groups:
            i = (c,) + g
            y = sa[g][c_len:] + av[i][c_len:] + _dot(a_rb[i], u2[g].astype(BF16))
            y_ref[g[0], pre[c, g[0]]["rows"], pcols[g[1]]] = y
    for gi, g in enumerate(groups):
        state[gi] = s_cur[g]


def _scan(r, lw, k, v, a, b, *, bsz, ts=256, n_seq=4):
    n, width = r.shape
    t_len = n // bsz
    ts = min(ts, t_len)
    n_seq = math.gcd(n_seq, bsz)
    n_heads = width // HEAD_DIM
    spec = pl.BlockSpec((n_seq, ts, width), lambda bb, t: (bb, t, 0))
    y = pl.pallas_call(
        functools.partial(_scan_body, n_heads=n_heads),
        grid=(bsz // n_seq, t_len // ts),
        in_specs=[spec] * 6,
        out_specs=spec,
        out_shape=jax.ShapeDtypeStruct((bsz, t_len, width), F32),
        scratch_shapes=[pltpu.VMEM((n_seq * width // LANES, LANES, LANES), F32)],
        compiler_params=_cparams("parallel", "arbitrary"),
        name="rwkv_scan",
    )(*(z.reshape(bsz, t_len, width) for z in (r, lw, k, v, a, b)))
    return y.reshape(n, width)


def _row(v):
    return v.reshape(1, -1)


def kernel(x, positions, ffn1_norm, ffn1_w_gate, ffn1_w_up, ffn1_w_down, mix_norm, ffn2_norm, ffn2_w_gate, ffn2_w_up, ffn2_w_down, ab_w_in, ab_mu_shift, rwkv_w0, rwkv_w_decay_up, rwkv_a0, rwkv_w_aaa_up, rwkv_w_gate_up, rwkv_k_k, rwkv_k_a, rwkv_r_k, rwkv_gn_gain, rwkv_gn_bias, sg_ln_gain, sg_ln_bias, sg_w_spatial, sg_b_spatial, ab_w_out, attn_w_qkv, attn_b_qkv, attn_q_norm, attn_k_norm, attn_sinks, attn_w_o, attn_b_o):
    bsz, t_len, d = x.shape
    depth = ffn1_norm.shape[0]
    a_width = rwkv_w0.shape[1]
    b_width = sg_ln_gain.shape[1]
    n_groups = sg_w_spatial.shape[1]
    assert a_width == b_width
    bf = lambda w: w.astype(BF16)

    xf = x.reshape(bsz * t_len, d)
    cos, sin = _rope_tables(positions.reshape(-1, 1))
    head_id = jnp.arange(2 * LANES) // HEAD_DIM
    ones_bd = (head_id[:, None] == head_id[None, :]).astype(BF16)
    pairs = LANES // HEAD_DIM
    ffn1 = [bf(w) for w in (ffn1_w_gate, ffn1_w_up, ffn1_w_down)]
    ffn2 = [bf(w) for w in (ffn2_w_gate, ffn2_w_up, ffn2_w_down)]
    w_in, w_out, w_qkv, w_o = bf(ab_w_in), bf(ab_w_out), bf(attn_w_qkv), bf(attn_w_o)

    for l in range(depth):
        i = l // 2
        xf = _ffn(xf, _row(ffn1_norm[l]), *(_layer(w, l) for w in ffn1))
        if l % 2 == 0:
            lora = rwkv_w_decay_up.shape[1]
            zeros = jnp.zeros((LANES - lora, a_width), F32)
            wdec = bf(jnp.concatenate([rwkv_w_decay_up[i], zeros], axis=0))
            waaa = bf(jnp.concatenate([zeros, rwkv_w_aaa_up[i]], axis=0))
            bsp = jnp.repeat(sg_b_spatial[i].T, b_width // n_groups, axis=1)
            r, lw, k, v, a, b, gate, bonus, yb = _mix_in(
                xf, _row(mix_norm[l]), _layer(w_in, i), _row(ab_mu_shift[i]), _row(rwkv_w0[i]), wdec,
                _row(rwkv_a0[i]), waaa, bf(rwkv_w_gate_up[i]), _row(rwkv_k_k[i]), _row(rwkv_k_a[i]),
                _row(rwkv_r_k[i]), _row(sg_ln_gain[i]), _row(sg_ln_bias[i]), sg_w_spatial[i], bsp, ones_bd,
                bsz=bsz)
            y = _scan(r, lw, k, v, a, b, bsz=bsz)
            xf = _mix_out_ffn(xf, y, bonus, gate, yb, _row(rwkv_gn_gain[i]), _row(rwkv_gn_bias[i]), ones_bd,
                              _layer(w_out, i, rows=a_width, row_block=0), _layer(w_out, i, rows=b_width, row_block=1),
                              _row(ffn2_norm[l]), *(_layer(w, l) for w in ffn2))
        else:
            xf = _attn(xf, cos, sin, _row(mix_norm[l]), _layer(w_qkv, i), _row(attn_b_qkv[i]),
                       _row(jnp.tile(attn_q_norm[i], pairs)), _row(jnp.tile(attn_k_norm[i], pairs)),
                       attn_sinks[i], _layer(w_o, i), _row(attn_b_o[i]), ones_bd, bsz=bsz)
            xf = _ffn(xf, _row(ffn2_norm[l]), *(_layer(w, l) for w in ffn2))
    return xf.reshape(bsz, t_len, d)
```

```python
import functools
import math
from typing import NamedTuple

import jax
import jax.numpy as jnp
from jax import lax
from jax.experimental import pallas as pl
from jax.experimental.pallas import tpu as pltpu

F32 = jnp.float32
BF16 = jnp.bfloat16

HEAD_DIM = 64
RMS_EPS = 1e-6
GN_EPS = 64e-5
LN_EPS = 1e-5
ROPE_THETA = 10000.0
LOG2_E = math.log2(math.e)
WINDOW = 128
SCAN_CHUNK = 64
LANES = 128
VMEM_BYTES_V7X = 64 * 1024 * 1024
VMEM_LIMIT = VMEM_BYTES_V7X * 7 // 8


def _cparams(*sem):
    return pltpu.CompilerParams(dimension_semantics=sem, vmem_limit_bytes=VMEM_LIMIT)


def _resident(shape):
    nd = len(shape)
    return pl.BlockSpec(shape, lambda *_: (0,) * nd, pipeline_mode=pl.Buffered(1))


class _LayerSlice(NamedTuple):
    array: jax.Array
    shape: tuple
    index: tuple


def _layer(array, layer, rows=None, row_block=0):
    shape = tuple(array.shape[1:])
    if rows is not None:
        shape = (rows,) + shape[1:]
    return _LayerSlice(array, shape, (layer, row_block, 0))


def _const_spec(c):
    if isinstance(c, _LayerSlice):
        return pl.BlockSpec((None,) + c.shape, lambda *_: c.index, pipeline_mode=pl.Buffered(1))
    return _resident(c.shape)


def _const_arg(c):
    return c.array if isinstance(c, _LayerSlice) else c


def _dot(a, b):
    return jnp.dot(a, b, preferred_element_type=F32)


def _dot_nt(a, b):
    return lax.dot_general(a, b, (((1,), (1,)), ((), ())), preferred_element_type=F32)


def _dot_tn(a, b):
    return lax.dot_general(a, b, (((0,), (0,)), ((), ())), preferred_element_type=F32)


def _rms(x, gain):
    return x * lax.rsqrt(jnp.mean(x * x, axis=-1, keepdims=True) + RMS_EPS) * gain


def _group_sum(x, ones_bd):
    xb = x.astype(BF16)
    bd = ones_bd.shape[0]
    slabs = [_dot(xb[:, c:c + bd], ones_bd) for c in range(0, x.shape[1], bd)]
    return slabs[0] if len(slabs) == 1 else jnp.concatenate(slabs, axis=1)


def _swiglu_half_step(x, gain_ref, wg_ref, wu_ref, wd_ref, n_chunks):
    h = _rms(x, gain_ref[...]).astype(BF16)
    fc = wg_ref.shape[1] // n_chunks

    def gate_up(c):
        return _dot(h, wg_ref[:, c * fc:(c + 1) * fc]), _dot(h, wu_ref[:, c * fc:(c + 1) * fc])

    acc = jnp.zeros_like(x)
    g, u = gate_up(0)
    for c in range(n_chunks):
        nxt = gate_up(c + 1) if c + 1 < n_chunks else None
        act = (g * jax.nn.sigmoid(g) * u).astype(BF16)
        acc = acc + _dot(act, wd_ref[c * fc:(c + 1) * fc, :])
        if nxt is not None:
            g, u = nxt
    return x + 0.5 * acc


def _ffn_body(x_ref, gain_ref, wg_ref, wu_ref, wd_ref, o_ref, *, n_chunks):
    o_ref[...] = _swiglu_half_step(x_ref[...], gain_ref, wg_ref, wu_ref, wd_ref, n_chunks)


def _mix_out_ffn_body(x_ref, y_ref, bonus_ref, gate_ref, yb_ref, gng_ref, gnb_ref, ones_ref, wa_ref, wb_ref,
                      gain_ref, wg_ref, wu_ref, wd_ref, o_ref, *, n_chunks):
    ones_bd = ones_ref[...]
    y = y_ref[...]
    mu = _group_sum(y, ones_bd) * (1.0 / HEAD_DIM)
    yc = y - mu
    var = _group_sum(yc * yc, ones_bd) * (1.0 / HEAD_DIM)
    yn = yc * lax.rsqrt(var + GN_EPS) * gng_ref[...] + gnb_ref[...]
    ya = ((yn + bonus_ref[...]) * gate_ref[...]).astype(BF16)
    x = x_ref[...] + _dot(ya, wa_ref[...]) + _dot(yb_ref[...], wb_ref[...])
    o_ref[...] = _swiglu_half_step(x, gain_ref, wg_ref, wu_ref, wd_ref, n_chunks)


def _ffn(x, gain, wg, wu, wd, *, tm=1024, n_chunks=11):
    n, d = x.shape
    tm = min(tm, n)
    return pl.pallas_call(
        functools.partial(_ffn_body, n_chunks=n_chunks),
        grid=(n // tm,),
        in_specs=[pl.BlockSpec((tm, d), lambda i: (i, 0))] + [_const_spec(c) for c in (gain, wg, wu, wd)],
        out_specs=pl.BlockSpec((tm, d), lambda i: (i, 0)),
        out_shape=jax.ShapeDtypeStruct((n, d), F32),
        compiler_params=_cparams("parallel"),
        name="ffn",
    )(x, *map(_const_arg, (gain, wg, wu, wd)))


def _mix_out_ffn(x, y, bonus, gate, yb, gng, gnb, ones_bd, wa, wb, gain, wg, wu, wd, *, tm=512, n_chunks=11):
    n, d = x.shape
    tm = min(tm, n)
    w = y.shape[1]
    xs = pl.BlockSpec((tm, d), lambda i: (i, 0))
    ws = pl.BlockSpec((tm, w), lambda i: (i, 0))
    consts = (gng, gnb, ones_bd, wa, wb, gain, wg, wu, wd)
    return pl.pallas_call(
        functools.partial(_mix_out_ffn_body, n_chunks=n_chunks),
        grid=(n // tm,),
        in_specs=[xs, ws, ws, ws, ws] + [_const_spec(c) for c in consts],
        out_specs=xs,
        out_shape=jax.ShapeDtypeStruct((n, d), F32),
        compiler_params=_cparams("parallel"),
        name="mix_out_ffn",
    )(x, y, bonus, gate, yb, *map(_const_arg, consts))


def _rope_body(pos_ref, freq_ref, cos_ref, sin_ref):
    ang = pos_ref[...].astype(F32) * freq_ref[...]
    lane = lax.broadcasted_iota(jnp.int32, ang.shape, 1)
    first_half = (lane % HEAD_DIM) < HEAD_DIM // 2
    cos_ref[...] = jnp.cos(ang)
    s = jnp.sin(ang)
    sin_ref[...] = jnp.where(first_half, -s, s)


def _rope_tables(pos, *, tm=2048):
    n = pos.shape[0]
    tm = min(tm, n)
    half = HEAD_DIM // 2
    inv_freq = ROPE_THETA ** (-jnp.arange(0, HEAD_DIM, 2, dtype=F32) / HEAD_DIM)
    freq = jnp.tile(inv_freq, LANES // half)[None, :]
    return pl.pallas_call(
        _rope_body,
        grid=(n // tm,),
        in_specs=[pl.BlockSpec((tm, 1), lambda i: (i, 0)), _resident((1, LANES))],
        out_specs=[pl.BlockSpec((tm, LANES), lambda i: (i, 0))] * 2,
        out_shape=[jax.ShapeDtypeStruct((n, LANES), F32)] * 2,
        compiler_params=_cparams("parallel"),
        name="rope_tables",
    )(pos, freq)


def _attn_body(sink_ref, x_ref, cos_ref, sin_ref, gain_ref, wqkv_ref, bqkv_ref, qg_ref, kg_ref, wo_ref, bo_ref,
               ones_ref, o_ref, k2_prev, v2_prev, *, n_q, n_kv, slab):
    t = pl.program_id(1)
    tq = x_ref.shape[0]
    nblk = tq // WINDOW
    qw = n_q * HEAD_DIM
    kw = n_kv * HEAD_DIM
    per_kv = n_q // n_kv

    @pl.when(t == 0)
    def _():
        k2_prev[...] = jnp.zeros_like(k2_prev)
        v2_prev[...] = jnp.zeros_like(v2_prev)

    h = _rms(x_ref[...], gain_ref[...]).astype(BF16)

    def proj(c0, c1):
        return _dot(h, wqkv_ref[:, c0:c1]) + bqkv_ref[:, c0:c1]

    lane = lax.broadcasted_iota(jnp.int32, (tq, LANES), 1)
    lo = lane < HEAD_DIM
    first_half = (lane % HEAD_DIM) < HEAD_DIM // 2
    cos = cos_ref[...]
    sin = sin_ref[...]

    ones_bd = ones_ref[...]

    def norm_rope(z, gain_pair, extra):
        inv = lax.rsqrt(_group_sum(z * z, ones_bd) * (1.0 / HEAD_DIM) + RMS_EPS) * extra
        out = []
        for p in range(z.shape[1] // LANES):
            pc = slice(p * LANES, (p + 1) * LANES)
            zg = z[:, pc] * gain_pair
            partner = jnp.where(first_half, pltpu.roll(zg, LANES - HEAD_DIM // 2, axis=1),
                                pltpu.roll(zg, HEAD_DIM // 2, axis=1))
            out.append((zg * cos + partner * sin) * inv[:, pc])
        return out

    scale = HEAD_DIM ** -0.5 * LOG2_E
    q_group = ones_bd.shape[0]
    kv = proj(qw, qw + 2 * kw)
    q_raw = [proj(0, q_group)]
    k_pairs = norm_rope(kv[:, :kw], kg_ref[...], 1.0)
    v_pairs = [kv[:, kw + p * LANES: kw + (p + 1) * LANES] for p in range(kw // LANES)]
    q_pairs = []
    for c in range(q_group, qw + q_group, q_group):
        if c < qw:
            q_raw.append(proj(c, c + q_group))
        q_pairs += [qp.astype(BF16) for qp in norm_rope(q_raw[c // q_group - 1], qg_ref[...], scale)]

    k2, v_top, v_bot = [], [], []
    for g in range(n_kv):
        kp = k_pairs[g // 2]
        vp = v_pairs[g // 2]
        kr = pltpu.roll(kp, HEAD_DIM, axis=1)
        vr = pltpu.roll(vp, HEAD_DIM, axis=1)
        if g % 2 == 0:
            k2.append(jnp.where(lo, kp, kr).astype(BF16))
            v_top.append(jnp.where(lo, vp, 0.0).astype(BF16))
            v_bot.append(jnp.where(lo, 0.0, vr).astype(BF16))
        else:
            k2.append(jnp.where(lo, kr, kp).astype(BF16))
            v_top.append(jnp.where(lo, vr, 0.0).astype(BF16))
            v_bot.append(jnp.where(lo, 0.0, vp).astype(BF16))

    qi = lax.broadcasted_iota(jnp.int32, (WINDOW, 2 * WINDOW), 0)
    kj = lax.broadcasted_iota(jnp.int32, (WINDOW, 2 * WINDOW), 1)
    dist = qi + WINDOW - kj
    in_window = (dist >= 0) & (dist < WINDOW)
    lo_blk = lax.broadcasted_iota(jnp.int32, (WINDOW, LANES), 1) < HEAD_DIM
    oc_row = lax.broadcasted_iota(jnp.int32, (4 * WINDOW, LANES), 0)
    oc_lane = lax.broadcasted_iota(jnp.int32, (4 * WINDOW, LANES), 1)
    ones_cols = ((oc_row < 2 * WINDOW) == (oc_lane < HEAD_DIM)).astype(BF16)

    n_pairs = per_kv // 2
    inst = [(blk, g, pp) for blk in range(nblk) for g in range(n_kv) for pp in range(n_pairs)]
    rows_of = lambda blk: slice(blk * WINDOW, (blk + 1) * WINDOW)
    valid_first = in_window & (kj >= jnp.where(t > 0, 0, WINDOW))
    bias = {0: jnp.where(valid_first, 0.0, -jnp.inf)}
    if nblk > 1:
        bias[1] = jnp.where(in_window, 0.0, -jnp.inf)
    lo_slab = lax.broadcasted_iota(jnp.int32, (slab, LANES), 1) < HEAD_DIM
    kband, vband = {}, {}
    for blk in range(nblk):
        for g in range(n_kv):
            if blk == 0:
                kprev, vtp, vbp = k2_prev[g], v2_prev[g, 0], v2_prev[g, 1]
            else:
                prows = rows_of(blk - 1)
                kprev, vtp, vbp = k2[g][prows], v_top[g][prows], v_bot[g][prows]
            rows = rows_of(blk)
            kband[blk, g] = jnp.concatenate([kprev, k2[g][rows]], axis=0)
            vals = jnp.concatenate([vtp, v_top[g][rows], vbp, v_bot[g][rows]], axis=0)
            vband[blk, g] = jnp.concatenate([vals, ones_cols], axis=1)
    def block_scores(b):
        out = {}
        for blk, g, pp in (i for i in inst if i[0] == b):
            qp = q_pairs[g * n_pairs + pp][rows_of(blk)]
            zero = jnp.zeros_like(qp)
            lhs = jnp.concatenate([jnp.where(lo_blk, qp, zero), jnp.where(lo_blk, zero, qp)], axis=0)
            out[blk, g, pp] = _dot_nt(lhs, kband[blk, g])
        return out

    for blk_now in range(nblk):
        wave_inst = [i for i in inst if i[0] == blk_now]
        o_pair = {}
        scores = block_scores(blk_now)
        p_cat, sink_term = {}, {}
        for blk, g, pp in wave_inst:
            pair = g * n_pairs + pp
            sc = scores[blk, g, pp]
            bias_blk = bias[min(blk, 1)]
            p_head, e_head = [], []
            for hd in range(2):
                sink = sink_ref[2 * pair + hd] * LOG2_E
                p_slabs, e_slabs = [], []
                for r0 in range(0, WINDOW, slab):
                    s = sc[hd * WINDOW + r0: hd * WINDOW + r0 + slab] + bias_blk[r0:r0 + slab]
                    m = jnp.maximum(jnp.max(s, axis=-1, keepdims=True), sink)
                    p_slabs.append(jnp.exp2(s - m).astype(BF16))
                    e_slabs.append(jnp.exp2(sink - m))
                p_head.append(jnp.concatenate(p_slabs, axis=0))
                e_head.append(e_slabs)
            p_cat[blk, g, pp] = jnp.concatenate(p_head, axis=1)
            sink_term[blk, g, pp] = jnp.concatenate(
                [jnp.where(lo_slab, e0, e1) for e0, e1 in zip(*e_head)], axis=0)
            pv = _dot(p_cat[blk, g, pp], vband[blk, g])
            o_pair[blk, g, pp] = pv[:, :LANES] / (pv[:, LANES:] + sink_term[blk, g, pp])
        o_blk = jnp.concatenate([o_pair[i] for i in wave_inst], axis=1).astype(BF16)
        rows = rows_of(blk_now)
        o_ref[rows, :] = x_ref[rows, :] + _dot(o_blk, wo_ref[...]) + bo_ref[...]

    last = rows_of(nblk - 1)
    for g in range(n_kv):
        k2_prev[g] = k2[g][last]
        v2_prev[g, 0] = v_top[g][last]
        v2_prev[g, 1] = v_bot[g][last]


def _attn(x, cos, sin, gain, wqkv, bqkv, qg, kg, sinks, wo, bo, ones_bd, *, bsz, tq=512, slab=32):
    n, d = x.shape
    t_len = n // bsz
    tq = min(tq, t_len)
    nt = t_len // tq
    n_q = wo.shape[0] // HEAD_DIM
    n_kv = (wqkv.shape[1] - wo.shape[0]) // (2 * HEAD_DIM)
    row = lambda b, t, *_: (b * nt + t, 0)
    consts = (gain, wqkv, bqkv, qg, kg, wo, bo, ones_bd)
    grid_spec = pltpu.PrefetchScalarGridSpec(
        num_scalar_prefetch=1,
        grid=(bsz, nt),
        in_specs=[pl.BlockSpec((tq, d), row), pl.BlockSpec((tq, LANES), row), pl.BlockSpec((tq, LANES), row)]
        + [_const_spec(c) for c in consts],
        out_specs=pl.BlockSpec((tq, d), row),
        scratch_shapes=[pltpu.VMEM((n_kv, WINDOW, LANES), BF16), pltpu.VMEM((n_kv, 2, WINDOW, LANES), BF16)],
    )
    return pl.pallas_call(
        functools.partial(_attn_body, n_q=n_q, n_kv=n_kv, slab=slab),
        grid_spec=grid_spec,
        out_shape=jax.ShapeDtypeStruct((n, d), F32),
        compiler_params=_cparams("parallel", "arbitrary"),
        name="swa_attention",
    )(sinks, x, cos, sin, *map(_const_arg, consts))


def _mix_in_body(x_ref, xp_ref, gain_ref, win_ref, mu_ref, w0_ref, wdec_ref, a0_ref, waaa_ref, wgate_ref,
                 kk_ref, ka_ref, rk_ref, lng_ref, lnb_ref, wsp_ref, bsp_ref, ones_ref,
                 r_o, lw_o, k_o, v_o, a_o, b_o, gate_o, bonus_o, yb_o, *, a_width, b_width):
    t = pl.program_id(1)
    tm = x_ref.shape[0]
    a_cols = mu_ref.shape[1]
    gain = gain_ref[...]
    h = _rms(x_ref[...], gain).astype(BF16)
    n_prev = xp_ref.shape[0]
    h_ext = jnp.concatenate([_rms(xp_ref[...], gain).astype(BF16), h], axis=0)
    first_of_seq = lax.broadcasted_iota(jnp.int32, (tm, 1), 0) == jnp.where(t == 0, 0, -1)

    def proj(c0, c1):
        return _dot(h, win_ref[:, c0:c1])

    def proj_ext(c0, c1):
        return _dot(h_ext, win_ref[:, c0:c1]), (c0, c1)

    def shift_mix(projected):
        z_ext, (c0, c1) = projected
        z = z_ext[n_prev:]
        shifted = jnp.where(first_of_seq, 0.0, pltpu.roll(z_ext, 1, axis=0)[n_prev:])
        return z + (shifted - z) * mu_ref[:, c0:c1]

    def gelu(z):
        return 0.5 * z * (1.0 + lax.erf(z * math.sqrt(0.5)))

    aw = a_width
    ones_bd = ones_ref[...]
    p_lora = proj_ext(3 * aw, a_cols)
    p_k = proj_ext(aw, 2 * aw)
    lora = shift_mix(p_lora)
    xwa = lora[:, :LANES]
    xg = lora[:, LANES:]
    y_dec = w0_ref[...] + _dot(jnp.tanh(xwa).astype(BF16), wdec_ref[...])
    lw_o[...] = (-math.exp(-0.5)) * jax.nn.sigmoid(y_dec)
    iclr = jax.nn.sigmoid(a0_ref[...] + _dot(xwa.astype(BF16), waaa_ref[...]))
    gate_o[...] = _dot(jax.nn.sigmoid(xg).astype(BF16), wgate_ref[...])
    p_r = proj_ext(0, aw)
    k = shift_mix(p_k)
    kk = k * kk_ref[...]
    kk = kk * lax.rsqrt(jnp.maximum(_group_sum(kk * kk, ones_bd), 1e-24))
    k = k * (1.0 + (iclr - 1.0) * ka_ref[...])
    k_o[...] = k
    a_o[...] = -kk
    b_o[...] = kk * iclr
    p_v = proj_ext(2 * aw, 3 * aw)
    r = shift_mix(p_r)
    r_o[...] = r
    rk = _group_sum(r * k * rk_ref[...], ones_bd)
    z_s = proj(a_cols + b_width, a_cols + 2 * b_width)
    v = shift_mix(p_v)
    v_o[...] = v
    bonus_o[...] = rk * v

    z_u = proj(a_cols, a_cols + b_width)
    s = gelu(z_s)
    u = gelu(z_u)
    mean = jnp.mean(s, axis=-1, keepdims=True)
    sc = s - mean
    var = jnp.mean(sc * sc, axis=-1, keepdims=True)
    s = (sc * lax.rsqrt(var + LN_EPS) * lng_ref[...] + lnb_ref[...]).astype(BF16)
    ti = lax.broadcasted_iota(jnp.int32, (WINDOW, WINDOW), 0)
    si = lax.broadcasted_iota(jnp.int32, (WINDOW, WINDOW), 1)
    n_groups = wsp_ref.shape[0]
    gd = b_width // n_groups
    for g in range(n_groups):
        w_g = jnp.where(si <= ti, wsp_ref[g], 0.0).astype(BF16)
        for c in range(tm // WINDOW):
            rows = slice(c * WINDOW, (c + 1) * WINDOW)
            cols = slice(g * gd, (g + 1) * gd)
            mixed = _dot(w_g, s[rows, cols]) + bsp_ref[:, cols]
            yb_o[rows, cols] = (u[rows, cols] * mixed).astype(yb_o.dtype)


def _mix_in(x, gain, win, mu, w0, wdec, a0, waaa, wgate, kk, ka, rk, lng, lnb, wsp, bsp, ones_bd, *, bsz, tm=512):
    n, d = x.shape
    t_len = n // bsz
    nt = t_len // tm
    a_width = w0.shape[1]
    b_width = lng.shape[1]
    prev_rows = 16
    row = lambda b, t: (b * nt + t, 0)
    prev = lambda b, t: (jnp.maximum((b * nt + t) * (tm // prev_rows) - 1, 0), 0)
    consts = [gain, win, mu, w0, wdec, a0, waaa, wgate, kk, ka, rk, lng, lnb, wsp, bsp, ones_bd]
    out_spec = pl.BlockSpec((tm, a_width), row)
    return pl.pallas_call(
        functools.partial(_mix_in_body, a_width=a_width, b_width=b_width),
        grid=(bsz, nt),
        in_specs=[pl.BlockSpec((tm, d), row), pl.BlockSpec((prev_rows, d), prev)] + [_const_spec(c) for c in consts],
        out_specs=[out_spec] * 9,
        out_shape=[jax.ShapeDtypeStruct((n, a_width), F32)] * 8 + [jax.ShapeDtypeStruct((n, b_width), BF16)],
        compiler_params=_cparams("parallel", "parallel"),
        name="mix_in",
    )(x, x, *map(_const_arg, consts))


def _scan_body(r_ref, lw_ref, k_ref, v_ref, a_ref, b_ref, y_ref, state, *, n_heads):
    c_len = SCAN_CHUNK
    assert c_len == HEAD_DIM and 2 * HEAD_DIM == LANES

    @pl.when(pl.program_id(1) == 0)
    def _():
        state[...] = jnp.zeros_like(state)

    two_c = 2 * c_len
    ti = lax.broadcasted_iota(jnp.int32, (c_len, c_len), 0)
    si = lax.broadcasted_iota(jnp.int32, (c_len, c_len), 1)
    tril = (si <= ti).astype(BF16)
    mid = c_len // 2 - 1
    row2 = lax.broadcasted_iota(jnp.int32, (two_c, two_c), 0)
    col2 = lax.broadcasted_iota(jnp.int32, (two_c, two_c), 1)
    t_idx, s_idx = row2 % c_len, col2 % c_len
    tri = (s_idx < t_idx) | ((row2 >= c_len) & (s_idx == t_idx))
    anti_eye = (t_idx == s_idx).astype(BF16)
    lo2 = col2 < HEAD_DIM
    same_head = (row2 < HEAD_DIM) == lo2
    lo1 = lax.broadcasted_iota(jnp.int32, (c_len, LANES), 1) < HEAD_DIM

    n_seq, ts, _ = r_ref.shape
    n_chunks = ts // c_len
    n_pairs = n_heads * HEAD_DIM // LANES
    pcols = [slice(p * LANES, (p + 1) * LANES) for p in range(n_pairs)]

    pre = {}
    for c in range(n_chunks):
        rows = slice(c * c_len, (c + 1) * c_len)
        for sq in range(n_seq):
            lw = lw_ref[sq, rows, :]
            lw_hi = lw.astype(BF16)
            lw_mid = (lw - lw_hi.astype(F32)).astype(BF16)
            lw_lo = (lw - lw_hi.astype(F32) - lw_mid.astype(F32)).astype(BF16)
            cum = _dot(tril, lw_hi) + _dot(tril, lw_mid) + _dot(tril, lw_lo)
            l_mid = cum[mid:mid + 1, :]
            l_end = cum[c_len - 1:, :]
            e_neg = jnp.exp(l_mid - cum)
            e_end = jnp.exp(l_end - cum)
            k, b = k_ref[sq, rows, :], b_ref[sq, rows, :]
            a_t = a_ref[sq, rows, :] * jnp.exp(cum - lw - l_mid)
            r_t = r_ref[sq, rows, :] * jnp.exp(cum - l_mid)
            p_mid = jnp.exp(l_mid)
            scaled = dict(
                a_t=a_t,
                r_t=r_t,
                a_s=a_t * p_mid,
                r_s=r_t * p_mid,
                b_t=b * e_neg,
                k_t=k * e_neg,
                b_e=b * e_end,
                k_e=k * e_end,
                v=v_ref[sq, rows, :])
            pre[c, sq] = dict({name: val.astype(BF16) for name, val in scaled.items()},
                              rows=rows, p_end=jnp.exp(l_end))

    groups = [(sq, p) for sq in range(n_seq) for p in range(n_pairs)]
    inst = [(c, sq, p) for c in range(n_chunks) for sq, p in groups]
    lhs_s, qkm, v_b, bk = {}, {}, {}, {}
    for c, sq, p in inst:
        pc, pr = pcols[p], pre[c, sq]
        lhs_b = jnp.concatenate([pr["a_t"][:, pc], pr["r_t"][:, pc]], axis=0)
        lhs_s[c, sq, p] = jnp.concatenate([pr["a_s"][:, pc], pr["r_s"][:, pc]], axis=0)
        bt = pr["b_t"][:, pc]
        kt = pr["k_t"][:, pc]
        qk0 = _dot_nt(jnp.where(lo2, lhs_b, 0), jnp.concatenate([kt, bt], axis=0))
        qk1 = _dot_nt(jnp.where(lo2, 0, lhs_b), jnp.concatenate([bt, kt], axis=0))
        qkm[c, sq, p] = (jnp.where(tri, qk0, 0.0).astype(BF16), jnp.where(tri, qk1, 0.0).astype(BF16))
        v_b[c, sq, p] = pr["v"][:, pc]
        bk[c, sq, p] = jnp.concatenate([pr["b_e"][:, pc], pr["k_e"][:, pc]], axis=0)
    a_rb, av, work = {}, {}, {}
    for i in inst:
        q0, q1 = qkm[i]
        a_rb[i] = jnp.where(lo1, q1[c_len:], q0[c_len:])
        v2 = jnp.concatenate([jnp.where(lo1, v_b[i], 0), jnp.where(lo1, 0, v_b[i])], axis=0)
        av[i] = _dot(jnp.where(lo2, q0, q1), v2)
        work[i] = jnp.where(same_head, jnp.concatenate([q1[:c_len], q0[:c_len]], axis=0), anti_eye)
    for _ in range(c_len.bit_length() - 1):
        for i in inst:
            prod = _dot(jnp.where(same_head, work[i], 0), work[i]).astype(BF16)
            work[i] = jnp.where(same_head, prod, work[i] + prod)
    inv = {i: jnp.where(same_head, 0, work[i]) for i in inst}

    s_cur = {g: state[gi] for gi, g in enumerate(groups)}
    for c in range(n_chunks):
        sa = {g: _dot_nt(lhs_s[(c,) + g], s_cur[g].astype(BF16)) for g in groups}
        u2 = {}
        for g in groups:
            x = (sa[g][:c_len] + av[(c,) + g][:c_len]).astype(BF16)
            rhs = jnp.concatenate([jnp.where(lo1, x, 0), jnp.where(lo1, 0, x)], axis=0)
            u2[g] = _dot(inv[(c,) + g], rhs)
        for g in groups:
            i = (c,) + g
            uv = jnp.concatenate([(u2[g][:c_len] + u2[g][c_len:]).astype(BF16), v_b[i]], axis=0)
            grown = jnp.where(same_head, _dot_tn(uv, bk[i]), 0.0)
            s_cur[g] = s_cur[g] * pre[c, g[0]]["p_end"][:, pcols[g[1]]] + grown
        for g in groups:
            i = (c,) + g
            y = sa[g][c_len:] + av[i][c_len:] + _dot(a_rb[i], u2[g].astype(BF16))
            y_ref[g[0], pre[c, g[0]]["rows"], pcols[g[1]]] = y
    for gi, g in enumerate(groups):
        state[gi] = s_cur[g]


def _scan(r, lw, k, v, a, b, *, bsz, ts=256, n_seq=4):
    n, width = r.shape
    t_len = n // bsz
    ts = min(ts, t_len)
    n_seq = math.gcd(n_seq, bsz)
    n_heads = width // HEAD_DIM
    spec = pl.BlockSpec((n_seq, ts, width), lambda bb, t: (bb, t, 0))
    y = pl.pallas_call(
        functools.partial(_scan_body, n_heads=n_heads),
        grid=(bsz // n_seq, t_len // ts),
        in_specs=[spec] * 6,
        out_specs=spec,
        out_shape=jax.ShapeDtypeStruct((bsz, t_len, width), F32),
        scratch_shapes=[pltpu.VMEM((n_seq * width // LANES, LANES, LANES), F32)],
        compiler_params=_cparams("parallel", "arbitrary"),
        name="rwkv_scan",
    )(*(z.reshape(bsz, t_len, width) for z in (r, lw, k, v, a, b)))
    return y.reshape(n, width)


def _row(v):
    return v.reshape(1, -1)


def kernel(x, positions, ffn1_norm, ffn1_w_gate, ffn1_w_up, ffn1_w_down, mix_norm, ffn2_norm, ffn2_w_gate, ffn2_w_up, ffn2_w_down, ab_w_in, ab_mu_shift, rwkv_w0, rwkv_w_decay_up, rwkv_a0, rwkv_w_aaa_up, rwkv_w_gate_up, rwkv_k_k, rwkv_k_a, rwkv_r_k, rwkv_gn_gain, rwkv_gn_bias, sg_ln_gain, sg_ln_bias, sg_w_spatial, sg_b_spatial, ab_w_out, attn_w_qkv, attn_b_qkv, attn_q_norm, attn_k_norm, attn_sinks, attn_w_o, attn_b_o):
    bsz, t_len, d = x.shape
    depth = ffn1_norm.shape[0]
    a_width = rwkv_w0.shape[1]
    b_width = sg_ln_gain.shape[1]
    n_groups = sg_w_spatial.shape[1]
    assert a_width == b_width
    bf = lambda w: w.astype(BF16)

    xf = x.reshape(bsz * t_len, d)
    cos, sin = _rope_tables(positions.reshape(-1, 1))
    head_id = jnp.arange(2 * LANES) // HEAD_DIM
    ones_bd = (head_id[:, None] == head_id[None, :]).astype(BF16)
    pairs = LANES // HEAD_DIM
    ffn1 = [bf(w) for w in (ffn1_w_gate, ffn1_w_up, ffn1_w_down)]
    ffn2 = [bf(w) for w in (ffn2_w_gate, ffn2_w_up, ffn2_w_down)]
    w_in, w_out, w_qkv, w_o = bf(ab_w_in), bf(ab_w_out), bf(attn_w_qkv), bf(attn_w_o)

    for l in range(depth):
        i = l // 2
        xf = _ffn(xf, _row(ffn1_norm[l]), *(_layer(w, l) for w in ffn1))
        if l % 2 == 0:
            lora = rwkv_w_decay_up.shape[1]
            zeros = jnp.zeros((LANES - lora, a_width), F32)
            wdec = bf(jnp.concatenate([rwkv_w_decay_up[i], zeros], axis=0))
            waaa = bf(jnp.concatenate([zeros, rwkv_w_aaa_up[i]], axis=0))
            bsp = jnp.repeat(sg_b_spatial[i].T, b_width // n_groups, axis=1)
            r, lw, k, v, a, b, gate, bonus, yb = _mix_in(
                xf, _row(mix_norm[l]), _layer(w_in, i), _row(ab_mu_shift[i]), _row(rwkv_w0[i]), wdec,
                _row(rwkv_a0[i]), waaa, bf(rwkv_w_gate_up[i]), _row(rwkv_k_k[i]), _row(rwkv_k_a[i]),
                _row(rwkv_r_k[i]), _row(sg_ln_gain[i]), _row(sg_ln_bias[i]), sg_w_spatial[i], bsp, ones_bd,
                bsz=bsz)
            y = _scan(r, lw, k, v, a, b, bsz=bsz)
            xf = _mix_out_ffn(xf, y, bonus, gate, yb, _row(rwkv_gn_gain[i]), _row(rwkv_gn_bias[i]), ones_bd,
                              _layer(w_out, i, rows=a_width, row_block=0), _layer(w_out, i, rows=b_width, row_block=1),
                              _row(ffn2_norm[l]), *(_layer(w, l) for w in ffn2))
        else:
            xf = _attn(xf, cos, sin, _row(mix_norm[l]), _layer(w_qkv, i), _row(attn_b_qkv[i]),
                       _row(jnp.tile(attn_q_norm[i], pairs)), _row(jnp.tile(attn_k_norm[i], pairs)),
                       attn_sinks[i], _layer(w_o, i), _row(attn_b_o[i]), ones_bd, bsz=bsz)
            xf = _ffn(xf, _row(ffn2_norm[l]), *(_layer(w, l) for w in ffn2))
    return xf.reshape(bsz, t_len, d)
```

```python
import functools
import math
from typing import NamedTuple

import jax
import jax.numpy as jnp
from jax import lax
from jax.experimental import pallas as pl
from jax.experimental.pallas import tpu as pltpu

F32 = jnp.float32
BF16 = jnp.bfloat16

HEAD_DIM = 64
RMS_EPS = 1e-6
GN_EPS = 64e-5
LN_EPS = 1e-5
ROPE_THETA = 10000.0
LOG2_E = math.log2(math.e)
WINDOW = 128
SCAN_CHUNK = 64
LANES = 128
VMEM_BYTES_V7X = 64 * 1024 * 1024
VMEM_LIMIT = VMEM_BYTES_V7X * 7 // 8


def _cparams(*sem):
    return pltpu.CompilerParams(dimension_semantics=sem, vmem_limit_bytes=VMEM_LIMIT)


def _resident(shape):
    nd = len(shape)
    return pl.BlockSpec(shape, lambda *_: (0,) * nd, pipeline_mode=pl.Buffered(1))


class _LayerSlice(NamedTuple):
    array: jax.Array
    shape: tuple
    index: tuple


def _layer(array, layer, rows=None, row_block=0):
    shape = tuple(array.shape[1:])
    if rows is not None:
        shape = (rows,) + shape[1:]
    return _LayerSlice(array, shape, (layer, row_block, 0))


def _const_spec(c):
    if isinstance(c, _LayerSlice):
        return pl.BlockSpec((None,) + c.shape, lambda *_: c.index, pipeline_mode=pl.Buffered(1))
    return _resident(c.shape)


def _const_arg(c):
    return c.array if isinstance(c, _LayerSlice) else c


def _dot(a, b):
    return jnp.dot(a, b, preferred_element_type=F32)


def _dot_nt(a, b):
    return lax.dot_general(a, b, (((1,), (1,)), ((), ())), preferred_element_type=F32)


def _dot_tn(a, b):
    return lax.dot_general(a, b, (((0,), (0,)), ((), ())), preferred_element_type=F32)


def _rms(x, gain):
    return x * lax.rsqrt(jnp.mean(x * x, axis=-1, keepdims=True) + RMS_EPS) * gain


def _group_sum(x, ones_bd):
    xb = x.astype(BF16)
    bd = ones_bd.shape[0]
    slabs = [_dot(xb[:, c:c + bd], ones_bd) for c in range(0, x.shape[1], bd)]
    return slabs[0] if len(slabs) == 1 else jnp.concatenate(slabs, axis=1)


def _swiglu_half_step(x, gain_ref, wg_ref, wu_ref, wd_ref, n_chunks):
    h = _rms(x, gain_ref[...]).astype(BF16)
    fc = wg_ref.shape[1] // n_chunks

    def gate_up(c):
        return _dot(h, wg_ref[:, c * fc:(c + 1) * fc]), _dot(h, wu_ref[:, c * fc:(c + 1) * fc])

    acc = jnp.zeros_like(x)
    g, u = gate_up(0)
    for c in range(n_chunks):
        nxt = gate_up(c + 1) if c + 1 < n_chunks else None
        act = (g * jax.nn.sigmoid(g) * u).astype(BF16)
        acc = acc + _dot(act, wd_ref[c * fc:(c + 1) * fc, :])
        if nxt is not None:
            g, u = nxt
    return x + 0.5 * acc


def _ffn_body(x_ref, gain_ref, wg_ref, wu_ref, wd_ref, o_ref, *, n_chunks):
    o_ref[...] = _swiglu_half_step(x_ref[...], gain_ref, wg_ref, wu_ref, wd_ref, n_chunks)


def _mix_out_ffn_body(x_ref, y_ref, bonus_ref, gate_ref, yb_ref, gng_ref, gnb_ref, ones_ref, wa_ref, wb_ref,
                      gain_ref, wg_ref, wu_ref, wd_ref, o_ref, *, n_chunks):
    ones_bd = ones_ref[...]
    y = y_ref[...]
    mu = _group_sum(y, ones_bd) * (1.0 / HEAD_DIM)
    yc = y - mu
    var = _group_sum(yc * yc, ones_bd) * (1.0 / HEAD_DIM)
    yn = yc * lax.rsqrt(var + GN_EPS) * gng_ref[...] + gnb_ref[...]
    ya = ((yn + bonus_ref[...]) * gate_ref[...]).astype(BF16)
    x = x_ref[...] + _dot(ya, wa_ref[...]) + _dot(yb_ref[...], wb_ref[...])
    o_ref[...] = _swiglu_half_step(x, gain_ref, wg_ref, wu_ref, wd_ref, n_chunks)


def _ffn(x, gain, wg, wu, wd, *, tm=1024, n_chunks=11):
    n, d = x.shape
    tm = min(tm, n)
    return pl.pallas_call(
        functools.partial(_ffn_body, n_chunks=n_chunks),
        grid=(n // tm,),
        in_specs=[pl.BlockSpec((tm, d), lambda i: (i, 0))] + [_const_spec(c) for c in (gain, wg, wu, wd)],
        out_specs=pl.BlockSpec((tm, d), lambda i: (i, 0)),
        out_shape=jax.ShapeDtypeStruct((n, d), F32),
        compiler_params=_cparams("parallel"),
        name="ffn",
    )(x, *map(_const_arg, (gain, wg, wu, wd)))


def _mix_out_ffn(x, y, bonus, gate, yb, gng, gnb, ones_bd, wa, wb, gain, wg, wu, wd, *, tm=512, n_chunks=11):
    n, d = x.shape
    tm = min(tm, n)
    w = y.shape[1]
    xs = pl.BlockSpec((tm, d), lambda i: (i, 0))
    ws = pl.BlockSpec((tm, w), lambda i: (i, 0))
    consts = (gng, gnb, ones_bd, wa, wb, gain, wg, wu, wd)
    return pl.pallas_call(
        functools.partial(_mix_out_ffn_body, n_chunks=n_chunks),
        grid=(n // tm,),
        in_specs=[xs, ws, ws, ws, ws] + [_const_spec(c) for c in consts],
        out_specs=xs,
        out_shape=jax.ShapeDtypeStruct((n, d), F32),
        compiler_params=_cparams("parallel"),
        name="mix_out_ffn",
    )(x, y, bonus, gate, yb, *map(_const_arg, consts))


def _rope_body(pos_ref, freq_ref, cos_ref, sin_ref):
    ang = pos_ref[...].astype(F32) * freq_ref[...]
    lane = lax.broadcasted_iota(jnp.int32, ang.shape, 1)
    first_half = (lane % HEAD_DIM) < HEAD_DIM // 2
    cos_ref[...] = jnp.cos(ang)
    s = jnp.sin(ang)
    sin_ref[...] = jnp.where(first_half, -s, s)


def _rope_tables(pos, *, tm=2048):
    n = pos.shape[0]
    tm = min(tm, n)
    half = HEAD_DIM // 2
    inv_freq = ROPE_THETA ** (-jnp.arange(0, HEAD_DIM, 2, dtype=F32) / HEAD_DIM)
    freq = jnp.tile(inv_freq, LANES // half)[None, :]
    return pl.pallas_call(
        _rope_body,
        grid=(n // tm,),
        in_specs=[pl.BlockSpec((tm, 1), lambda i: (i, 0)), _resident((1, LANES))],
        out_specs=[pl.BlockSpec((tm, LANES), lambda i: (i, 0))] * 2,
        out_shape=[jax.ShapeDtypeStruct((n, LANES), F32)] * 2,
        compiler_params=_cparams("parallel"),
        name="rope_tables",
    )(pos, freq)


def _attn_body(sink_ref, x_ref, cos_ref, sin_ref, gain_ref, wqkv_ref, bqkv_ref, qg_ref, kg_ref, wo_ref, bo_ref,
               ones_ref, o_ref, k2_prev, v2_prev, *, n_q, n_kv, slab):
    t = pl.program_id(1)
    tq = x_ref.shape[0]
    nblk = tq // WINDOW
    qw = n_q * HEAD_DIM
    kw = n_kv * HEAD_DIM
    per_kv = n_q // n_kv

    @pl.when(t == 0)
    def _():
        k2_prev[...] = jnp.zeros_like(k2_prev)
        v2_prev[...] = jnp.zeros_like(v2_prev)

    h = _rms(x_ref[...], gain_ref[...]).astype(BF16)

    def proj(c0, c1):
        return _dot(h, wqkv_ref[:, c0:c1]) + bqkv_ref[:, c0:c1]

    lane = lax.broadcasted_iota(jnp.int32, (tq, LANES), 1)
    lo = lane < HEAD_DIM
    first_half = (lane % HEAD_DIM) < HEAD_DIM // 2
    cos = cos_ref[...]
    sin = sin_ref[...]

    ones_bd = ones_ref[...]

    def norm_rope(z, gain_pair, extra):
        inv = lax.rsqrt(_group_sum(z * z, ones_bd) * (1.0 / HEAD_DIM) + RMS_EPS) * extra
        out = []
        for p in range(z.shape[1] // LANES):
            pc = slice(p * LANES, (p + 1) * LANES)
            zg = z[:, pc] * gain_pair
            partner = jnp.where(first_half, pltpu.roll(zg, LANES - HEAD_DIM // 2, axis=1),
                                pltpu.roll(zg, HEAD_DIM // 2, axis=1))
            out.append((zg * cos + partner * sin) * inv[:, pc])
        return out

    scale = HEAD_DIM ** -0.5 * LOG2_E
    q_group = ones_bd.shape[0]
    kv = proj(qw, qw + 2 * kw)
    q_raw = [proj(0, q_group)]
    k_pairs = norm_rope(kv[:, :kw], kg_ref[...], 1.0)
    v_pairs = [kv[:, kw + p * LANES: kw + (p + 1) * LANES] for p in range(kw // LANES)]
    q_pairs = []
    for c in range(q_group, qw + q_group, q_group):
        if c < qw:
            q_raw.append(proj(c, c + q_group))
        q_pairs += [qp.astype(BF16) for qp in norm_rope(q_raw[c // q_group - 1], qg_ref[...], scale)]

    k2, v_top, v_bot = [], [], []
    for g in range(n_kv):
        kp = k_pairs[g // 2]
        vp = v_pairs[g // 2]
        kr = pltpu.roll(kp, HEAD_DIM, axis=1)
        vr = pltpu.roll(vp, HEAD_DIM, axis=1)
        if g % 2 == 0:
            k2.append(jnp.where(lo, kp, kr).astype(BF16))
            v_top.append(jnp.where(lo, vp, 0.0).astype(BF16))
            v_bot.append(jnp.where(lo, 0.0, vr).astype(BF16))
        else:
            k2.append(jnp.where(lo, kr, kp).astype(BF16))
            v_top.append(jnp.where(lo, vr, 0.0).astype(BF16))
            v_bot.append(jnp.where(lo, 0.0, vp).astype(BF16))

    qi = lax.broadcasted_iota(jnp.int32, (WINDOW, 2 * WINDOW), 0)
    kj = lax.broadcasted_iota(jnp.int32, (WINDOW, 2 * WINDOW), 1)
    dist = qi + WINDOW - kj
    in_window = (dist >= 0) & (dist < WINDOW)
    lo_blk = lax.broadcasted_iota(jnp.int32, (WINDOW, LANES), 1) < HEAD_DIM
    oc_row = lax.broadcasted_iota(jnp.int32, (4 * WINDOW, LANES), 0)
    oc_lane = lax.broadcasted_iota(jnp.int32, (4 * WINDOW, LANES), 1)
    ones_cols = ((oc_row < 2 * WINDOW) == (oc_lane < HEAD_DIM)).astype(BF16)

    n_pairs = per_kv // 2
    inst = [(blk, g, pp) for blk in range(nblk) for g in range(n_kv) for pp in range(n_pairs)]
    rows_of = lambda blk: slice(blk * WINDOW, (blk + 1) * WINDOW)
    valid_first = in_window & (kj >= jnp.where(t > 0, 0, WINDOW))
    bias = {0: jnp.where(valid_first, 0.0, -jnp.inf)}
    if nblk > 1:
        bias[1] = jnp.where(in_window, 0.0, -jnp.inf)
    lo_slab = lax.broadcasted_iota(jnp.int32, (slab, LANES), 1) < HEAD_DIM
    kband, vband = {}, {}
    for blk in range(nblk):
        for g in range(n_kv):
            if blk == 0:
                kprev, vtp, vbp = k2_prev[g], v2_prev[g, 0], v2_prev[g, 1]
            else:
                prows = rows_of(blk - 1)
                kprev, vtp, vbp = k2[g][prows], v_top[g][prows], v_bot[g][prows]
            rows = rows_of(blk)
            kband[blk, g] = jnp.concatenate([kprev, k2[g][rows]], axis=0)
            vals = jnp.concatenate([vtp, v_top[g][rows], vbp, v_bot[g][rows]], axis=0)
            vband[blk, g] = jnp.concatenate([vals, ones_cols], axis=1)
    def block_scores(b):
        out = {}
        for blk, g, pp in (i for i in inst if i[0] == b):
            qp = q_pairs[g * n_pairs + pp][rows_of(blk)]
            zero = jnp.zeros_like(qp)
            lhs = jnp.concatenate([jnp.where(lo_blk, qp, zero), jnp.where(lo_blk, zero, qp)], axis=0)
            out[blk, g, pp] = _dot_nt(lhs, kband[blk, g])
        return out

    for blk_now in range(nblk):
        wave_inst = [i for i in inst if i[0] == blk_now]
        o_pair = {}
        scores = block_scores(blk_now)
        p_cat, sink_term = {}, {}
        for blk, g, pp in wave_inst:
            pair = g * n_pairs + pp
            sc = scores[blk, g, pp]
            bias_blk = bias[min(blk, 1)]
            p_head, e_head = [], []
            for hd in range(2):
                sink = sink_ref[2 * pair + hd] * LOG2_E
                p_slabs, e_slabs = [], []
                for r0 in range(0, WINDOW, slab):
                    s = sc[hd * WINDOW + r0: hd * WINDOW + r0 + slab] + bias_blk[r0:r0 + slab]
                    m = jnp.maximum(jnp.max(s, axis=-1, keepdims=True), sink)
                    p_slabs.append(jnp.exp2(s - m).astype(BF16))
                    e_slabs.append(jnp.exp2(sink - m))
                p_head.append(jnp.concatenate(p_slabs, axis=0))
                e_head.append(e_slabs)
            p_cat[blk, g, pp] = jnp.concatenate(p_head, axis=1)
            sink_term[blk, g, pp] = jnp.concatenate(
                [jnp.where(lo_slab, e0, e1) for e0, e1 in zip(*e_head)], axis=0)
            pv = _dot(p_cat[blk, g, pp], vband[blk, g])
            o_pair[blk, g, pp] = pv[:, :LANES] / (pv[:, LANES:] + sink_term[blk, g, pp])
        o_blk = jnp.concatenate([o_pair[i] for i in wave_inst], axis=1).astype(BF16)
        rows = rows_of(blk_now)
        o_ref[rows, :] = x_ref[rows, :] + _dot(o_blk, wo_ref[...]) + bo_ref[...]

    last = rows_of(nblk - 1)
    for g in range(n_kv):
        k2_prev[g] = k2[g][last]
        v2_prev[g, 0] = v_top[g][last]
        v2_prev[g, 1] = v_bot[g][last]


def _attn(x, cos, sin, gain, wqkv, bqkv, qg, kg, sinks, wo, bo, ones_bd, *, bsz, tq=512, slab=32):
    n, d = x.shape
    t_len = n // bsz
    tq = min(tq, t_len)
    nt = t_len // tq
    n_q = wo.shape[0] // HEAD_DIM
    n_kv = (wqkv.shape[1] - wo.shape[0]) // (2 * HEAD_DIM)
    row = lambda b, t, *_: (b * nt + t, 0)
    consts = (gain, wqkv, bqkv, qg, kg, wo, bo, ones_bd)
    grid_spec = pltpu.PrefetchScalarGridSpec(
        num_scalar_prefetch=1,
        grid=(bsz, nt),
        in_specs=[pl.BlockSpec((tq, d), row), pl.BlockSpec((tq, LANES), row), pl.BlockSpec((tq, LANES), row)]
        + [_const_spec(c) for c in consts],
        out_specs=pl.BlockSpec((tq, d), row),
        scratch_shapes=[pltpu.VMEM((n_kv, WINDOW, LANES), BF16), pltpu.VMEM((n_kv, 2, WINDOW, LANES), BF16)],
    )
    return pl.pallas_call(
        functools.partial(_attn_body, n_q=n_q, n_kv=n_kv, slab=slab),
        grid_spec=grid_spec,
        out_shape=jax.ShapeDtypeStruct((n, d), F32),
        compiler_params=_cparams("parallel", "arbitrary"),
        name="swa_attention",
    )(sinks, x, cos, sin, *map(_const_arg, consts))


def _mix_in_body(x_ref, xp_ref, gain_ref, win_ref, mu_ref, w0_ref, wdec_ref, a0_ref, waaa_ref, wgate_ref,
                 kk_ref, ka_ref, rk_ref, lng_ref, lnb_ref, wsp_ref, bsp_ref, ones_ref,
                 r_o, lw_o, k_o, v_o, a_o, b_o, gate_o, bonus_o, yb_o, *, a_width, b_width):
    t = pl.program_id(1)
    tm = x_ref.shape[0]
    a_cols = mu_ref.shape[1]
    gain = gain_ref[...]
    h = _rms(x_ref[...], gain).astype(BF16)
    n_prev = xp_ref.shape[0]
    h_ext = jnp.concatenate([_rms(xp_ref[...], gain).astype(BF16), h], axis=0)
    first_of_seq = lax.broadcasted_iota(jnp.int32, (tm, 1), 0) == jnp.where(t == 0, 0, -1)

    def proj(c0, c1):
        return _dot(h, win_ref[:, c0:c1])

    def proj_ext(c0, c1):
        return _dot(h_ext, win_ref[:, c0:c1]), (c0, c1)

    def shift_mix(projected):
        z_ext, (c0, c1) = projected
        z = z_ext[n_prev:]
        shifted = jnp.where(first_of_seq, 0.0, pltpu.roll(z_ext, 1, axis=0)[n_prev:])
        return z + (shifted - z) * mu_ref[:, c0:c1]

    def gelu(z):
        return 0.5 * z * (1.0 + lax.erf(z * math.sqrt(0.5)))

    aw = a_width
    ones_bd = ones_ref[...]
    p_lora = proj_ext(3 * aw, a_cols)
    p_k = proj_ext(aw, 2 * aw)
    lora = shift_mix(p_lora)
    xwa = lora[:, :LANES]
    xg = lora[:, LANES:]
    y_dec = w0_ref[...] + _dot(jnp.tanh(xwa).astype(BF16), wdec_ref[...])
    lw_o[...] = (-math.exp(-0.5)) * jax.nn.sigmoid(y_dec)
    iclr = jax.nn.sigmoid(a0_ref[...] + _dot(xwa.astype(BF16), waaa_ref[...]))
    gate_o[...] = _dot(jax.nn.sigmoid(xg).astype(BF16), wgate_ref[...])
    p_r = proj_ext(0, aw)
    k = shift_mix(p_k)
    kk = k * kk_ref[...]
    kk = kk * lax.rsqrt(jnp.maximum(_group_sum(kk * kk, ones_bd), 1e-24))
    k = k * (1.0 + (iclr - 1.0) * ka_ref[...])
    k_o[...] = k
    a_o[...] = -kk
    b_o[...] = kk * iclr
    p_v = proj_ext(2 * aw, 3 * aw)
    r = shift_mix(p_r)
    r_o[...] = r
    rk = _group_sum(r * k * rk_ref[...], ones_bd)
    z_s = proj(a_cols + b_width, a_cols + 2 * b_width)
    v = shift_mix(p_v)
    v_o[...] = v
    bonus_o[...] = rk * v

    z_u = proj(a_cols, a_cols + b_width)
    s = gelu(z_s)
    u = gelu(z_u)
    mean = jnp.mean(s, axis=-1, keepdims=True)
    sc = s - mean
    var = jnp.mean(sc * sc, axis=-1, keepdims=True)
    s = (sc * lax.rsqrt(var + LN_EPS) * lng_ref[...] + lnb_ref[...]).astype(BF16)
    ti = lax.broadcasted_iota(jnp.int32, (WINDOW, WINDOW), 0)
    si = lax.broadcasted_iota(jnp.int32, (WINDOW, WINDOW), 1)
    n_groups = wsp_ref.shape[0]
    gd = b_width // n_groups
    for g in range(n_groups):
        w_g = jnp.where(si <= ti, wsp_ref[g], 0.0).astype(BF16)
        for c in range(tm // WINDOW):
            rows = slice(c * WINDOW, (c + 1) * WINDOW)
            cols = slice(g * gd, (g + 1) * gd)
            mixed = _dot(w_g, s[rows, cols]) + bsp_ref[:, cols]
            yb_o[rows, cols] = (u[rows, cols] * mixed).astype(yb_o.dtype)


def _mix_in(x, gain, win, mu, w0, wdec, a0, waaa, wgate, kk, ka, rk, lng, lnb, wsp, bsp, ones_bd, *, bsz, tm=512):
    n, d = x.shape
    t_len = n // bsz
    nt = t_len // tm
    a_width = w0.shape[1]
    b_width = lng.shape[1]
    prev_rows = 16
    row = lambda b, t: (b * nt + t, 0)
    prev = lambda b, t: (jnp.maximum((b * nt + t) * (tm // prev_rows) - 1, 0), 0)
    consts = [gain, win, mu, w0, wdec, a0, waaa, wgate, kk, ka, rk, lng, lnb, wsp, bsp, ones_bd]
    out_spec = pl.BlockSpec((tm, a_width), row)
    return pl.pallas_call(
        functools.partial(_mix_in_body, a_width=a_width, b_width=b_width),
        grid=(bsz, nt),
        in_specs=[pl.BlockSpec((tm, d), row), pl.BlockSpec((prev_rows, d), prev)] + [_const_spec(c) for c in consts],
        out_specs=[out_spec] * 9,
        out_shape=[jax.ShapeDtypeStruct((n, a_width), F32)] * 8 + [jax.ShapeDtypeStruct((n, b_width), BF16)],
        compiler_params=_cparams("parallel", "parallel"),
        name="mix_in",
    )(x, x, *map(_const_arg, consts))


def _scan_body(r_ref, lw_ref, k_ref, v_ref, a_ref, b_ref, y_ref, state, *, n_heads):
    c_len = SCAN_CHUNK
    assert c_len == HEAD_DIM and 2 * HEAD_DIM == LANES

    @pl.when(pl.program_id(1) == 0)
    def _():
        state[...] = jnp.zeros_like(state)

    two_c = 2 * c_len
    ti = lax.broadcasted_iota(jnp.int32, (c_len, c_len), 0)
    si = lax.broadcasted_iota(jnp.int32, (c_len, c_len), 1)
    tril = (si <= ti).astype(BF16)
    mid = c_len // 2 - 1
    row2 = lax.broadcasted_iota(jnp.int32, (two_c, two_c), 0)
    col2 = lax.broadcasted_iota(jnp.int32, (two_c, two_c), 1)
    t_idx, s_idx = row2 % c_len, col2 % c_len
    tri = (s_idx < t_idx) | ((row2 >= c_len) & (s_idx == t_idx))
    anti_eye = (t_idx == s_idx).astype(BF16)
    lo2 = col2 < HEAD_DIM
    same_head = (row2 < HEAD_DIM) == lo2
    lo1 = lax.broadcasted_iota(jnp.int32, (c_len, LANES), 1) < HEAD_DIM

    n_seq, ts, _ = r_ref.shape
    n_chunks = ts // c_len
    n_pairs = n_heads * HEAD_DIM // LANES
    pcols = [slice(p * LANES, (p + 1) * LANES) for p in range(n_pairs)]

    pre = {}

    def decay_scaled_operands(c):
        rows = slice(c * c_len, (c + 1) * c_len)
        for sq in range(n_seq):
            lw = lw_ref[sq, rows, :]
            lw_hi = lw.astype(BF16)
            lw_mid = (lw - lw_hi.astype(F32)).astype(BF16)
            lw_lo = (lw - lw_hi.astype(F32) - lw_mid.astype(F32)).astype(BF16)
            cum = _dot(tril, lw_hi) + _dot(tril, lw_mid) + _dot(tril, lw_lo)
            l_mid = cum[mid:mid + 1, :]
            l_end = cum[c_len - 1:, :]
            e_neg = jnp.exp(l_mid - cum)
            e_end = jnp.exp(l_end - cum)
            k, b = k_ref[sq, rows, :], b_ref[sq, rows, :]
            a_t = a_ref[sq, rows, :] * jnp.exp(cum - lw - l_mid)
            r_t = r_ref[sq, rows, :] * jnp.exp(cum - l_mid)
            p_mid = jnp.exp(l_mid)
            scaled = dict(
                a_t=a_t,
                r_t=r_t,
                a_s=a_t * p_mid,
                r_s=r_t * p_mid,
                b_t=b * e_neg,
                k_t=k * e_neg,
                b_e=b * e_end,
                k_e=k * e_end,
                v=v_ref[sq, rows, :])
            pre[c, sq] = dict({name: val.astype(BF16) for name, val in scaled.items()},
                              rows=rows, p_end=jnp.exp(l_end))

    groups = [(sq, p) for sq in range(n_seq) for p in range(n_pairs)]
    inst = [(c, sq, p) for c in range(n_chunks) for sq, p in groups]
    lhs_s, v_b, bk, a_rb, av, inv = {}, {}, {}, {}, {}, {}

    def state_free_part(wave_inst):
        qkm, work = {}, {}
        for c, sq, p in wave_inst:
            pc, pr = pcols[p], pre[c, sq]
            lhs_b = jnp.concatenate([pr["a_t"][:, pc], pr["r_t"][:, pc]], axis=0)
            lhs_s[c, sq, p] = jnp.concatenate([pr["a_s"][:, pc], pr["r_s"][:, pc]], axis=0)
            bt = pr["b_t"][:, pc]
            kt = pr["k_t"][:, pc]
            qk0 = _dot_nt(jnp.where(lo2, lhs_b, 0), jnp.concatenate([kt, bt], axis=0))
            qk1 = _dot_nt(jnp.where(lo2, 0, lhs_b), jnp.concatenate([bt, kt], axis=0))
            qkm[c, sq, p] = (jnp.where(tri, qk0, 0.0).astype(BF16), jnp.where(tri, qk1, 0.0).astype(BF16))
            v_b[c, sq, p] = pr["v"][:, pc]
            bk[c, sq, p] = jnp.concatenate([pr["b_e"][:, pc], pr["k_e"][:, pc]], axis=0)
        for i in wave_inst:
            q0, q1 = qkm[i]
            a_rb[i] = jnp.where(lo1, q1[c_len:], q0[c_len:])
            v2 = jnp.concatenate([jnp.where(lo1, v_b[i], 0), jnp.where(lo1, 0, v_b[i])], axis=0)
            av[i] = _dot(jnp.where(lo2, q0, q1), v2)
            work[i] = jnp.where(same_head, jnp.concatenate([q1[:c_len], q0[:c_len]], axis=0), anti_eye)
        for _ in range(c_len.bit_length() - 1):
            for i in wave_inst:
                prod = _dot(jnp.where(same_head, work[i], 0), work[i]).astype(BF16)
                work[i] = jnp.where(same_head, prod, work[i] + prod)
        for i in wave_inst:
            inv[i] = jnp.where(same_head, 0, work[i])

    s_cur = {g: state[gi] for gi, g in enumerate(groups)}
    for c in range(n_chunks):
        decay_scaled_operands(c)
        state_free_part([i for i in inst if i[0] == c])
        sa = {g: _dot_nt(lhs_s[(c,) + g], s_cur[g].astype(BF16)) for g in groups}
        u2 = {}
        for g in groups:
            x = (sa[g][:c_len] + av[(c,) + g][:c_len]).astype(BF16)
            rhs = jnp.concatenate([jnp.where(lo1, x, 0), jnp.where(lo1, 0, x)], axis=0)
            u2[g] = _dot(inv[(c,) + g], rhs)
        for g in groups:
            i = (c,) + g
            uv = jnp.concatenate([(u2[g][:c_len] + u2[g][c_len:]).astype(BF16), v_b[i]], axis=0)
            grown = jnp.where(same_head, _dot_tn(uv, bk[i]), 0.0)
            s_cur[g] = s_cur[g] * pre[c, g[0]]["p_end"][:, pcols[g[1]]] + grown
        for g in groups:
            i = (c,) + g
            y = sa[g][c_len:] + av[i][c_len:] + _dot(a_rb[i], u2[g].astype(BF16))
            y_ref[g[0], pre[c, g[0]]["rows"], pcols[g[1]]] = y
    for gi, g in enumerate(groups):
        state[gi] = s_cur[g]


def _scan(r, lw, k, v, a, b, *, bsz, ts=256, n_seq=4):
    n, width = r.shape
    t_len = n // bsz
    ts = min(ts, t_len)
    n_seq = math.gcd(n_seq, bsz)
    n_heads = width // HEAD_DIM
    spec = pl.BlockSpec((n_seq, ts, width), lambda bb, t: (bb, t, 0))
    y = pl.pallas_call(
        functools.partial(_scan_body, n_heads=n_heads),
        grid=(bsz // n_seq, t_len // ts),
        in_specs=[spec] * 6,
        out_specs=spec,
        out_shape=jax.ShapeDtypeStruct((bsz, t_len, width), F32),
        scratch_shapes=[pltpu.VMEM((n_seq * width // LANES, LANES, LANES), F32)],
        compiler_params=_cparams("parallel", "arbitrary"),
        name="rwkv_scan",
    )(*(z.reshape(bsz, t_len, width) for z in (r, lw, k, v, a, b)))
    return y.reshape(n, width)


def _row(v):
    return v.reshape(1, -1)


def kernel(x, positions, ffn1_norm, ffn1_w_gate, ffn1_w_up, ffn1_w_down, mix_norm, ffn2_norm, ffn2_w_gate, ffn2_w_up, ffn2_w_down, ab_w_in, ab_mu_shift, rwkv_w0, rwkv_w_decay_up, rwkv_a0, rwkv_w_aaa_up, rwkv_w_gate_up, rwkv_k_k, rwkv_k_a, rwkv_r_k, rwkv_gn_gain, rwkv_gn_bias, sg_ln_gain, sg_ln_bias, sg_w_spatial, sg_b_spatial, ab_w_out, attn_w_qkv, attn_b_qkv, attn_q_norm, attn_k_norm, attn_sinks, attn_w_o, attn_b_o):
    bsz, t_len, d = x.shape
    depth = ffn1_norm.shape[0]
    a_width = rwkv_w0.shape[1]
    b_width = sg_ln_gain.shape[1]
    n_groups = sg_w_spatial.shape[1]
    assert a_width == b_width
    bf = lambda w: w.astype(BF16)

    xf = x.reshape(bsz * t_len, d)
    cos, sin = _rope_tables(positions.reshape(-1, 1))
    head_id = jnp.arange(2 * LANES) // HEAD_DIM
    ones_bd = (head_id[:, None] == head_id[None, :]).astype(BF16)
    pairs = LANES // HEAD_DIM
    ffn1 = [bf(w) for w in (ffn1_w_gate, ffn1_w_up, ffn1_w_down)]
    ffn2 = [bf(w) for w in (ffn2_w_gate, ffn2_w_up, ffn2_w_down)]
    w_in, w_out, w_qkv, w_o = bf(ab_w_in), bf(ab_w_out), bf(attn_w_qkv), bf(attn_w_o)

    for l in range(depth):
        i = l // 2
        xf = _ffn(xf, _row(ffn1_norm[l]), *(_layer(w, l) for w in ffn1))
        if l % 2 == 0:
            lora = rwkv_w_decay_up.shape[1]
            zeros = jnp.zeros((LANES - lora, a_width), F32)
            wdec = bf(jnp.concatenate([rwkv_w_decay_up[i], zeros], axis=0))
            waaa = bf(jnp.concatenate([zeros, rwkv_w_aaa_up[i]], axis=0))
            bsp = jnp.repeat(sg_b_spatial[i].T, b_width // n_groups, axis=1)
            r, lw, k, v, a, b, gate, bonus, yb = _mix_in(
                xf, _row(mix_norm[l]), _layer(w_in, i), _row(ab_mu_shift[i]), _row(rwkv_w0[i]), wdec,
                _row(rwkv_a0[i]), waaa, bf(rwkv_w_gate_up[i]), _row(rwkv_k_k[i]), _row(rwkv_k_a[i]),
                _row(rwkv_r_k[i]), _row(sg_ln_gain[i]), _row(sg_ln_bias[i]), sg_w_spatial[i], bsp, ones_bd,
                bsz=bsz)
            y = _scan(r, lw, k, v, a, b, bsz=bsz)
            xf = _mix_out_ffn(xf, y, bonus, gate, yb, _row(rwkv_gn_gain[i]), _row(rwkv_gn_bias[i]), ones_bd,
                              _layer(w_out, i, rows=a_width, row_block=0), _layer(w_out, i, rows=b_width, row_block=1),
                              _row(ffn2_norm[l]), *(_layer(w, l) for w in ffn2))
        else:
            xf = _attn(xf, cos, sin, _row(mix_norm[l]), _layer(w_qkv, i), _row(attn_b_qkv[i]),
                       _row(jnp.tile(attn_q_norm[i], pairs)), _row(jnp.tile(attn_k_norm[i], pairs)),
                       attn_sinks[i], _layer(w_o, i), _row(attn_b_o[i]), ones_bd, bsz=bsz)
            xf = _ffn(xf, _row(ffn2_norm[l]), *(_layer(w, l) for w in ffn2))
    return xf.reshape(bsz, t_len, d)
```

```python
import functools
import math
from typing import NamedTuple

import jax
import jax.numpy as jnp
from jax import lax
from jax.experimental import pallas as pl
from jax.experimental.pallas import tpu as pltpu

F32 = jnp.float32
BF16 = jnp.bfloat16

HEAD_DIM = 64
RMS_EPS = 1e-6
GN_EPS = 64e-5
LN_EPS = 1e-5
ROPE_THETA = 10000.0
LOG2_E = math.log2(math.e)
WINDOW = 128
SCAN_CHUNK = 64
LANES = 128
VMEM_BYTES_V7X = 64 * 1024 * 1024
VMEM_LIMIT = VMEM_BYTES_V7X * 7 // 8


def _cparams(*sem):
    return pltpu.CompilerParams(dimension_semantics=sem, vmem_limit_bytes=VMEM_LIMIT)


def _resident(shape):
    nd = len(shape)
    return pl.BlockSpec(shape, lambda *_: (0,) * nd, pipeline_mode=pl.Buffered(1))


class _LayerSlice(NamedTuple):
    array: jax.Array
    shape: tuple
    index: tuple


def _layer(array, layer, rows=None, row_block=0):
    shape = tuple(array.shape[1:])
    if rows is not None:
        shape = (rows,) + shape[1:]
    return _LayerSlice(array, shape, (layer, row_block, 0))


def _const_spec(c):
    if isinstance(c, _LayerSlice):
        return pl.BlockSpec((None,) + c.shape, lambda *_: c.index, pipeline_mode=pl.Buffered(1))
    return _resident(c.shape)


def _const_arg(c):
    return c.array if isinstance(c, _LayerSlice) else c


def _dot(a, b):
    return jnp.dot(a, b, preferred_element_type=F32)


def _dot_nt(a, b):
    return lax.dot_general(a, b, (((1,), (1,)), ((), ())), preferred_element_type=F32)


def _dot_tn(a, b):
    return lax.dot_general(a, b, (((0,), (0,)), ((), ())), preferred_element_type=F32)


def _rms(x, gain):
    return x * lax.rsqrt(jnp.mean(x * x, axis=-1, keepdims=True) + RMS_EPS) * gain


def _group_sum(x, ones_bd):
    xb = x.astype(BF16)
    bd = ones_bd.shape[0]
    slabs = [_dot(xb[:, c:c + bd], ones_bd) for c in range(0, x.shape[1], bd)]
    return slabs[0] if len(slabs) == 1 else jnp.concatenate(slabs, axis=1)


def _swiglu_half_step(x, gain_ref, wg_ref, wu_ref, wd_ref, n_chunks):
    h = _rms(x, gain_ref[...]).astype(BF16)
    fc = wg_ref.shape[1] // n_chunks

    def gate_up(c):
        return _dot(h, wg_ref[:, c * fc:(c + 1) * fc]), _dot(h, wu_ref[:, c * fc:(c + 1) * fc])

    acc = jnp.zeros_like(x)
    g, u = gate_up(0)
    for c in range(n_chunks):
        nxt = gate_up(c + 1) if c + 1 < n_chunks else None
        act = (g * jax.nn.sigmoid(g) * u).astype(BF16)
        acc = acc + _dot(act, wd_ref[c * fc:(c + 1) * fc, :])
        if nxt is not None:
            g, u = nxt
    return x + 0.5 * acc


def _ffn_body(x_ref, gain_ref, wg_ref, wu_ref, wd_ref, o_ref, *, n_chunks):
    o_ref[...] = _swiglu_half_step(x_ref[...], gain_ref, wg_ref, wu_ref, wd_ref, n_chunks)


def _mix_out_ffn_body(x_ref, y_ref, bonus_ref, gate_ref, yb_ref, gng_ref, gnb_ref, ones_ref, wa_ref, wb_ref,
                      gain_ref, wg_ref, wu_ref, wd_ref, o_ref, *, n_chunks):
    ones_bd = ones_ref[...]
    y = y_ref[...]
    mu = _group_sum(y, ones_bd) * (1.0 / HEAD_DIM)
    yc = y - mu
    var = _group_sum(yc * yc, ones_bd) * (1.0 / HEAD_DIM)
    yn = yc * lax.rsqrt(var + GN_EPS) * gng_ref[...] + gnb_ref[...]
    ya = ((yn + bonus_ref[...]) * gate_ref[...]).astype(BF16)
    x = x_ref[...] + _dot(ya, wa_ref[...]) + _dot(yb_ref[...], wb_ref[...])
    o_ref[...] = _swiglu_half_step(x, gain_ref, wg_ref, wu_ref, wd_ref, n_chunks)


def _ffn(x, gain, wg, wu, wd, *, tm=1024, n_chunks=11):
    n, d = x.shape
    tm = min(tm, n)
    return pl.pallas_call(
        functools.partial(_ffn_body, n_chunks=n_chunks),
        grid=(n // tm,),
        in_specs=[pl.BlockSpec((tm, d), lambda i: (i, 0))] + [_const_spec(c) for c in (gain, wg, wu, wd)],
        out_specs=pl.BlockSpec((tm, d), lambda i: (i, 0)),
        out_shape=jax.ShapeDtypeStruct((n, d), F32),
        compiler_params=_cparams("parallel"),
        name="ffn",
    )(x, *map(_const_arg, (gain, wg, wu, wd)))


def _mix_out_ffn(x, y, bonus, gate, yb, gng, gnb, ones_bd, wa, wb, gain, wg, wu, wd, *, tm=512, n_chunks=11):
    n, d = x.shape
    tm = min(tm, n)
    w = y.shape[1]
    xs = pl.BlockSpec((tm, d), lambda i: (i, 0))
    ws = pl.BlockSpec((tm, w), lambda i: (i, 0))
    consts = (gng, gnb, ones_bd, wa, wb, gain, wg, wu, wd)
    return pl.pallas_call(
        functools.partial(_mix_out_ffn_body, n_chunks=n_chunks),
        grid=(n // tm,),
        in_specs=[xs, ws, ws, ws, ws] + [_const_spec(c) for c in consts],
        out_specs=xs,
        out_shape=jax.ShapeDtypeStruct((n, d), F32),
        compiler_params=_cparams("parallel"),
        name="mix_out_ffn",
    )(x, y, bonus, gate, yb, *map(_const_arg, consts))


def _rope_body(pos_ref, freq_ref, cos_ref, sin_ref):
    ang = pos_ref[...].astype(F32) * freq_ref[...]
    lane = lax.broadcasted_iota(jnp.int32, ang.shape, 1)
    first_half = (lane % HEAD_DIM) < HEAD_DIM // 2
    cos_ref[...] = jnp.cos(ang)
    s = jnp.sin(ang)
    sin_ref[...] = jnp.where(first_half, -s, s)


def _rope_tables(pos, *, tm=2048):
    n = pos.shape[0]
    tm = min(tm, n)
    half = HEAD_DIM // 2
    inv_freq = ROPE_THETA ** (-jnp.arange(0, HEAD_DIM, 2, dtype=F32) / HEAD_DIM)
    freq = jnp.tile(inv_freq, LANES // half)[None, :]
    return pl.pallas_call(
        _rope_body,
        grid=(n // tm,),
        in_specs=[pl.BlockSpec((tm, 1), lambda i: (i, 0)), _resident((1, LANES))],
        out_specs=[pl.BlockSpec((tm, LANES), lambda i: (i, 0))] * 2,
        out_shape=[jax.ShapeDtypeStruct((n, LANES), F32)] * 2,
        compiler_params=_cparams("parallel"),
        name="rope_tables",
    )(pos, freq)


def _attn_body(sink_ref, x_ref, cos_ref, sin_ref, gain_ref, wqkv_ref, bqkv_ref, qg_ref, kg_ref, wo_ref, bo_ref,
               ones_ref, o_ref, k2_prev, v2_prev, *, n_q, n_kv, slab):
    t = pl.program_id(1)
    tq = x_ref.shape[0]
    nblk = tq // WINDOW
    qw = n_q * HEAD_DIM
    kw = n_kv * HEAD_DIM
    per_kv = n_q // n_kv

    @pl.when(t == 0)
    def _():
        k2_prev[...] = jnp.zeros_like(k2_prev)
        v2_prev[...] = jnp.zeros_like(v2_prev)

    h = _rms(x_ref[...], gain_ref[...]).astype(BF16)

    def proj(c0, c1):
        return _dot(h, wqkv_ref[:, c0:c1]) + bqkv_ref[:, c0:c1]

    lane = lax.broadcasted_iota(jnp.int32, (tq, LANES), 1)
    lo = lane < HEAD_DIM
    first_half = (lane % HEAD_DIM) < HEAD_DIM // 2
    cos = cos_ref[...]
    sin = sin_ref[...]

    ones_bd = ones_ref[...]

    def norm_rope(z, gain_pair, extra):
        inv = lax.rsqrt(_group_sum(z * z, ones_bd) * (1.0 / HEAD_DIM) + RMS_EPS) * extra
        out = []
        for p in range(z.shape[1] // LANES):
            pc = slice(p * LANES, (p + 1) * LANES)
            zg = z[:, pc] * gain_pair
            partner = jnp.where(first_half, pltpu.roll(zg, LANES - HEAD_DIM // 2, axis=1),
                                pltpu.roll(zg, HEAD_DIM // 2, axis=1))
            out.append((zg * cos + partner * sin) * inv[:, pc])
        return out

    scale = HEAD_DIM ** -0.5 * LOG2_E
    q_group = ones_bd.shape[0]
    kv = proj(qw, qw + 2 * kw)
    q_raw = [proj(0, q_group)]
    k_pairs = norm_rope(kv[:, :kw], kg_ref[...], 1.0)
    v_pairs = [kv[:, kw + p * LANES: kw + (p + 1) * LANES] for p in range(kw // LANES)]
    q_pairs = []
    for c in range(q_group, qw + q_group, q_group):
        if c < qw:
            q_raw.append(proj(c, c + q_group))
        q_pairs += [qp.astype(BF16) for qp in norm_rope(q_raw[c // q_group - 1], qg_ref[...], scale)]

    k2, v_top, v_bot = [], [], []
    for g in range(n_kv):
        kp = k_pairs[g // 2]
        vp = v_pairs[g // 2]
        kr = pltpu.roll(kp, HEAD_DIM, axis=1)
        vr = pltpu.roll(vp, HEAD_DIM, axis=1)
        if g % 2 == 0:
            k2.append(jnp.where(lo, kp, kr).astype(BF16))
            v_top.append(jnp.where(lo, vp, 0.0).astype(BF16))
            v_bot.append(jnp.where(lo, 0.0, vr).astype(BF16))
        else:
            k2.append(jnp.where(lo, kr, kp).astype(BF16))
            v_top.append(jnp.where(lo, vr, 0.0).astype(BF16))
            v_bot.append(jnp.where(lo, 0.0, vp).astype(BF16))

    qi = lax.broadcasted_iota(jnp.int32, (WINDOW, 2 * WINDOW), 0)
    kj = lax.broadcasted_iota(jnp.int32, (WINDOW, 2 * WINDOW), 1)
    dist = qi + WINDOW - kj
    in_window = (dist >= 0) & (dist < WINDOW)
    lo_blk = lax.broadcasted_iota(jnp.int32, (WINDOW, LANES), 1) < HEAD_DIM
    oc_row = lax.broadcasted_iota(jnp.int32, (4 * WINDOW, LANES), 0)
    oc_lane = lax.broadcasted_iota(jnp.int32, (4 * WINDOW, LANES), 1)
    ones_cols = ((oc_row < 2 * WINDOW) == (oc_lane < HEAD_DIM)).astype(BF16)

    n_pairs = per_kv // 2
    inst = [(blk, g, pp) for blk in range(nblk) for g in range(n_kv) for pp in range(n_pairs)]
    rows_of = lambda blk: slice(blk * WINDOW, (blk + 1) * WINDOW)
    valid_first = in_window & (kj >= jnp.where(t > 0, 0, WINDOW))
    bias = {0: jnp.where(valid_first, 0.0, -jnp.inf)}
    if nblk > 1:
        bias[1] = jnp.where(in_window, 0.0, -jnp.inf)
    lo_slab = lax.broadcasted_iota(jnp.int32, (slab, LANES), 1) < HEAD_DIM
    kband, vband = {}, {}
    for blk in range(nblk):
        for g in range(n_kv):
            if blk == 0:
                kprev, vtp, vbp = k2_prev[g], v2_prev[g, 0], v2_prev[g, 1]
            else:
                prows = rows_of(blk - 1)
                kprev, vtp, vbp = k2[g][prows], v_top[g][prows], v_bot[g][prows]
            rows = rows_of(blk)
            kband[blk, g] = jnp.concatenate([kprev, k2[g][rows]], axis=0)
            vals = jnp.concatenate([vtp, v_top[g][rows], vbp, v_bot[g][rows]], axis=0)
            vband[blk, g] = jnp.concatenate([vals, ones_cols], axis=1)
    def block_scores(b):
        out = {}
        for blk, g, pp in (i for i in inst if i[0] == b):
            qp = q_pairs[g * n_pairs + pp][rows_of(blk)]
            zero = jnp.zeros_like(qp)
            lhs = jnp.concatenate([jnp.where(lo_blk, qp, zero), jnp.where(lo_blk, zero, qp)], axis=0)
            out[blk, g, pp] = _dot_nt(lhs, kband[blk, g])
        return out

    for blk_now in range(nblk):
        wave_inst = [i for i in inst if i[0] == blk_now]
        o_pair = {}
        scores = block_scores(blk_now)
        p_cat, sink_term = {}, {}
        for blk, g, pp in wave_inst:
            pair = g * n_pairs + pp
            sc = scores[blk, g, pp]
            bias_blk = bias[min(blk, 1)]
            p_head, e_head = [], []
            for hd in range(2):
                sink = sink_ref[2 * pair + hd] * LOG2_E
                p_slabs, e_slabs = [], []
                for r0 in range(0, WINDOW, slab):
                    s = sc[hd * WINDOW + r0: hd * WINDOW + r0 + slab] + bias_blk[r0:r0 + slab]
                    m = jnp.maximum(jnp.max(s, axis=-1, keepdims=True), sink)
                    p_slabs.append(jnp.exp2(s - m).astype(BF16))
                    e_slabs.append(jnp.exp2(sink - m))
                p_head.append(jnp.concatenate(p_slabs, axis=0))
                e_head.append(e_slabs)
            p_cat[blk, g, pp] = jnp.concatenate(p_head, axis=1)
            sink_term[blk, g, pp] = jnp.concatenate(
                [jnp.where(lo_slab, e0, e1) for e0, e1 in zip(*e_head)], axis=0)
            pv = _dot(p_cat[blk, g, pp], vband[blk, g])
            o_pair[blk, g, pp] = pv[:, :LANES] / (pv[:, LANES:] + sink_term[blk, g, pp])
        o_blk = jnp.concatenate([o_pair[i] for i in wave_inst], axis=1).astype(BF16)
        rows = rows_of(blk_now)
        o_ref[rows, :] = x_ref[rows, :] + _dot(o_blk, wo_ref[...]) + bo_ref[...]

    last = rows_of(nblk - 1)
    for g in range(n_kv):
        k2_prev[g] = k2[g][last]
        v2_prev[g, 0] = v_top[g][last]
        v2_prev[g, 1] = v_bot[g][last]


def _attn(x, cos, sin, gain, wqkv, bqkv, qg, kg, sinks, wo, bo, ones_bd, *, bsz, tq=1024, slab=32):
    n, d = x.shape
    t_len = n // bsz
    tq = min(tq, t_len)
    nt = t_len // tq
    n_q = wo.shape[0] // HEAD_DIM
    n_kv = (wqkv.shape[1] - wo.shape[0]) // (2 * HEAD_DIM)
    row = lambda b, t, *_: (b * nt + t, 0)
    consts = (gain, wqkv, bqkv, qg, kg, wo, bo, ones_bd)
    grid_spec = pltpu.PrefetchScalarGridSpec(
        num_scalar_prefetch=1,
        grid=(bsz, nt),
        in_specs=[pl.BlockSpec((tq, d), row), pl.BlockSpec((tq, LANES), row), pl.BlockSpec((tq, LANES), row)]
        + [_const_spec(c) for c in consts],
        out_specs=pl.BlockSpec((tq, d), row),
        scratch_shapes=[pltpu.VMEM((n_kv, WINDOW, LANES), BF16), pltpu.VMEM((n_kv, 2, WINDOW, LANES), BF16)],
    )
    return pl.pallas_call(
        functools.partial(_attn_body, n_q=n_q, n_kv=n_kv, slab=slab),
        grid_spec=grid_spec,
        out_shape=jax.ShapeDtypeStruct((n, d), F32),
        compiler_params=_cparams("parallel", "arbitrary"),
        name="swa_attention",
    )(sinks, x, cos, sin, *map(_const_arg, consts))


def _mix_in_body(x_ref, xp_ref, gain_ref, win_ref, mu_ref, w0_ref, wdec_ref, a0_ref, waaa_ref, wgate_ref,
                 kk_ref, ka_ref, rk_ref, lng_ref, lnb_ref, wsp_ref, bsp_ref, ones_ref,
                 r_o, lw_o, k_o, v_o, a_o, b_o, gate_o, bonus_o, yb_o, *, a_width, b_width):
    t = pl.program_id(1)
    tm = x_ref.shape[0]
    a_cols = mu_ref.shape[1]
    gain = gain_ref[...]
    h = _rms(x_ref[...], gain).astype(BF16)
    n_prev = xp_ref.shape[0]
    h_ext = jnp.concatenate([_rms(xp_ref[...], gain).astype(BF16), h], axis=0)
    first_of_seq = lax.broadcasted_iota(jnp.int32, (tm, 1), 0) == jnp.where(t == 0, 0, -1)

    def proj(c0, c1):
        return _dot(h, win_ref[:, c0:c1])

    def proj_ext(c0, c1):
        return _dot(h_ext, win_ref[:, c0:c1]), (c0, c1)

    def shift_mix(projected):
        z_ext, (c0, c1) = projected
        z = z_ext[n_prev:]
        shifted = jnp.where(first_of_seq, 0.0, pltpu.roll(z_ext, 1, axis=0)[n_prev:])
        return z + (shifted - z) * mu_ref[:, c0:c1]

    def gelu(z):
        return 0.5 * z * (1.0 + lax.erf(z * math.sqrt(0.5)))

    aw = a_width
    ones_bd = ones_ref[...]
    p_lora = proj_ext(3 * aw, a_cols)
    p_k = proj_ext(aw, 2 * aw)
    lora = shift_mix(p_lora)
    xwa = lora[:, :LANES]
    xg = lora[:, LANES:]
    y_dec = w0_ref[...] + _dot(jnp.tanh(xwa).astype(BF16), wdec_ref[...])
    lw_o[...] = (-math.exp(-0.5)) * jax.nn.sigmoid(y_dec)
    iclr = jax.nn.sigmoid(a0_ref[...] + _dot(xwa.astype(BF16), waaa_ref[...]))
    gate_o[...] = _dot(jax.nn.sigmoid(xg).astype(BF16), wgate_ref[...])
    p_r = proj_ext(0, aw)
    k = shift_mix(p_k)
    kk = k * kk_ref[...]
    kk = kk * lax.rsqrt(jnp.maximum(_group_sum(kk * kk, ones_bd), 1e-24))
    k = k * (1.0 + (iclr - 1.0) * ka_ref[...])
    k_o[...] = k
    a_o[...] = -kk
    b_o[...] = kk * iclr
    p_v = proj_ext(2 * aw, 3 * aw)
    r = shift_mix(p_r)
    r_o[...] = r
    rk = _group_sum(r * k * rk_ref[...], ones_bd)
    z_s = proj(a_cols + b_width, a_cols + 2 * b_width)
    v = shift_mix(p_v)
    v_o[...] = v
    bonus_o[...] = rk * v

    z_u = proj(a_cols, a_cols + b_width)
    s = gelu(z_s)
    u = gelu(z_u)
    mean = jnp.mean(s, axis=-1, keepdims=True)
    sc = s - mean
    var = jnp.mean(sc * sc, axis=-1, keepdims=True)
    s = (sc * lax.rsqrt(var + LN_EPS) * lng_ref[...] + lnb_ref[...]).astype(BF16)
    ti = lax.broadcasted_iota(jnp.int32, (WINDOW, WINDOW), 0)
    si = lax.broadcasted_iota(jnp.int32, (WINDOW, WINDOW), 1)
    n_groups = wsp_ref.shape[0]
    gd = b_width // n_groups
    for g in range(n_groups):
        w_g = jnp.where(si <= ti, wsp_ref[g], 0.0).astype(BF16)
        for c in range(tm // WINDOW):
            rows = slice(c * WINDOW, (c + 1) * WINDOW)
            cols = slice(g * gd, (g + 1) * gd)
            mixed = _dot(w_g, s[rows, cols]) + bsp_ref[:, cols]
            yb_o[rows, cols] = (u[rows, cols] * mixed).astype(yb_o.dtype)


def _mix_in(x, gain, win, mu, w0, wdec, a0, waaa, wgate, kk, ka, rk, lng, lnb, wsp, bsp, ones_bd, *, bsz, tm=512):
    n, d = x.shape
    t_len = n // bsz
    nt = t_len // tm
    a_width = w0.shape[1]
    b_width = lng.shape[1]
    prev_rows = 16
    row = lambda b, t: (b * nt + t, 0)
    prev = lambda b, t: (jnp.maximum((b * nt + t) * (tm // prev_rows) - 1, 0), 0)
    consts = [gain, win, mu, w0, wdec, a0, waaa, wgate, kk, ka, rk, lng, lnb, wsp, bsp, ones_bd]
    out_spec = pl.BlockSpec((tm, a_width), row)
    return pl.pallas_call(
        functools.partial(_mix_in_body, a_width=a_width, b_width=b_width),
        grid=(bsz, nt),
        in_specs=[pl.BlockSpec((tm, d), row), pl.BlockSpec((prev_rows, d), prev)] + [_const_spec(c) for c in consts],
        out_specs=[out_spec] * 9,
        out_shape=[jax.ShapeDtypeStruct((n, a_width), F32)] * 8 + [jax.ShapeDtypeStruct((n, b_width), BF16)],
        compiler_params=_cparams("parallel", "parallel"),
        name="mix_in",
    )(x, x, *map(_const_arg, consts))


def _scan_body(r_ref, lw_ref, k_ref, v_ref, a_ref, b_ref, y_ref, state, *, n_heads):
    c_len = SCAN_CHUNK
    assert c_len == HEAD_DIM and 2 * HEAD_DIM == LANES

    @pl.when(pl.program_id(1) == 0)
    def _():
        state[...] = jnp.zeros_like(state)

    two_c = 2 * c_len
    ti = lax.broadcasted_iota(jnp.int32, (c_len, c_len), 0)
    si = lax.broadcasted_iota(jnp.int32, (c_len, c_len), 1)
    tril = (si <= ti).astype(BF16)
    mid = c_len // 2 - 1
    row2 = lax.broadcasted_iota(jnp.int32, (two_c, two_c), 0)
    col2 = lax.broadcasted_iota(jnp.int32, (two_c, two_c), 1)
    t_idx, s_idx = row2 % c_len, col2 % c_len
    tri = (s_idx < t_idx) | ((row2 >= c_len) & (s_idx == t_idx))
    anti_eye = (t_idx == s_idx).astype(BF16)
    lo2 = col2 < HEAD_DIM
    same_head = (row2 < HEAD_DIM) == lo2
    lo1 = lax.broadcasted_iota(jnp.int32, (c_len, LANES), 1) < HEAD_DIM

    n_seq, ts, _ = r_ref.shape
    n_chunks = ts // c_len
    n_pairs = n_heads * HEAD_DIM // LANES
    pcols = [slice(p * LANES, (p + 1) * LANES) for p in range(n_pairs)]

    pre = {}

    def decay_scaled_operands(c):
        rows = slice(c * c_len, (c + 1) * c_len)
        for sq in range(n_seq):
            lw = lw_ref[sq, rows, :]
            lw_hi = lw.astype(BF16)
            lw_mid = (lw - lw_hi.astype(F32)).astype(BF16)
            lw_lo = (lw - lw_hi.astype(F32) - lw_mid.astype(F32)).astype(BF16)
            cum = _dot(tril, lw_hi) + _dot(tril, lw_mid) + _dot(tril, lw_lo)
            l_mid = cum[mid:mid + 1, :]
            l_end = cum[c_len - 1:, :]
            e_neg = jnp.exp(l_mid - cum)
            e_end = jnp.exp(l_end - cum)
            k, b = k_ref[sq, rows, :], b_ref[sq, rows, :]
            a_t = a_ref[sq, rows, :] * jnp.exp(cum - lw - l_mid)
            r_t = r_ref[sq, rows, :] * jnp.exp(cum - l_mid)
            p_mid = jnp.exp(l_mid)
            scaled = dict(
                a_t=a_t,
                r_t=r_t,
                a_s=a_t * p_mid,
                r_s=r_t * p_mid,
                b_t=b * e_neg,
                k_t=k * e_neg,
                b_e=b * e_end,
                k_e=k * e_end,
                v=v_ref[sq, rows, :])
            pre[c, sq] = dict({name: val.astype(BF16) for name, val in scaled.items()},
                              rows=rows, p_end=jnp.exp(l_end))

    groups = [(sq, p) for sq in range(n_seq) for p in range(n_pairs)]
    inst = [(c, sq, p) for c in range(n_chunks) for sq, p in groups]
    lhs_s, v_b, bk, a_rb, av, inv = {}, {}, {}, {}, {}, {}

    def state_free_part(wave_inst):
        qkm, work = {}, {}
        for c, sq, p in wave_inst:
            pc, pr = pcols[p], pre[c, sq]
            lhs_b = jnp.concatenate([pr["a_t"][:, pc], pr["r_t"][:, pc]], axis=0)
            lhs_s[c, sq, p] = jnp.concatenate([pr["a_s"][:, pc], pr["r_s"][:, pc]], axis=0)
            bt = pr["b_t"][:, pc]
            kt = pr["k_t"][:, pc]
            qk0 = _dot_nt(jnp.where(lo2, lhs_b, 0), jnp.concatenate([kt, bt], axis=0))
            qk1 = _dot_nt(jnp.where(lo2, 0, lhs_b), jnp.concatenate([bt, kt], axis=0))
            qkm[c, sq, p] = (jnp.where(tri, qk0, 0.0).astype(BF16), jnp.where(tri, qk1, 0.0).astype(BF16))
            v_b[c, sq, p] = pr["v"][:, pc]
            bk[c, sq, p] = jnp.concatenate([pr["b_e"][:, pc], pr["k_e"][:, pc]], axis=0)
        for i in wave_inst:
            q0, q1 = qkm[i]
            a_rb[i] = jnp.where(lo1, q1[c_len:], q0[c_len:])
            v2 = jnp.concatenate([jnp.where(lo1, v_b[i], 0), jnp.where(lo1, 0, v_b[i])], axis=0)
            av[i] = _dot(jnp.where(lo2, q0, q1), v2)
            work[i] = jnp.where(same_head, jnp.concatenate([q1[:c_len], q0[:c_len]], axis=0), anti_eye)
        for _ in range(c_len.bit_length() - 1):
            for i in wave_inst:
                prod = _dot(jnp.where(same_head, work[i], 0), work[i]).astype(BF16)
                work[i] = jnp.where(same_head, prod, work[i] + prod)
        for i in wave_inst:
            inv[i] = jnp.where(same_head, 0, work[i])

    s_cur = {g: state[gi] for gi, g in enumerate(groups)}
    for c in range(n_chunks):
        decay_scaled_operands(c)
        state_free_part([i for i in inst if i[0] == c])
        sa = {g: _dot_nt(lhs_s[(c,) + g], s_cur[g].astype(BF16)) for g in groups}
        u2 = {}
        for g in groups:
            x = (sa[g][:c_len] + av[(c,) + g][:c_len]).astype(BF16)
            rhs = jnp.concatenate([jnp.where(lo1, x, 0), jnp.where(lo1, 0, x)], axis=0)
            u2[g] = _dot(inv[(c,) + g], rhs)
        for g in groups:
            i = (c,) + g
            uv = jnp.concatenate([(u2[g][:c_len] + u2[g][c_len:]).astype(BF16), v_b[i]], axis=0)
            grown = jnp.where(same_head, _dot_tn(uv, bk[i]), 0.0)
            s_cur[g] = s_cur[g] * pre[c, g[0]]["p_end"][:, pcols[g[1]]] + grown
        for g in groups:
            i = (c,) + g
            y = sa[g][c_len:] + av[i][c_len:] + _dot(a_rb[i], u2[g].astype(BF16))
            y_ref[g[0], pre[c, g[0]]["rows"], pcols[g[1]]] = y
    for gi, g in enumerate(groups):
        state[gi] = s_cur[g]


def _scan(r, lw, k, v, a, b, *, bsz, ts=256, n_seq=4):
    n, width = r.shape
    t_len = n // bsz
    ts = min(ts, t_len)
    n_seq = math.gcd(n_seq, bsz)
    n_heads = width // HEAD_DIM
    spec = pl.BlockSpec((n_seq, ts, width), lambda bb, t: (bb, t, 0))
    y = pl.pallas_call(
        functools.partial(_scan_body, n_heads=n_heads),
        grid=(bsz // n_seq, t_len // ts),
        in_specs=[spec] * 6,
        out_specs=spec,
        out_shape=jax.ShapeDtypeStruct((bsz, t_len, width), F32),
        scratch_shapes=[pltpu.VMEM((n_seq * width // LANES, LANES, LANES), F32)],
        compiler_params=_cparams("parallel", "arbitrary"),
        name="rwkv_scan",
    )(*(z.reshape(bsz, t_len, width) for z in (r, lw, k, v, a, b)))
    return y.reshape(n, width)


def _row(v):
    return v.reshape(1, -1)


def kernel(x, positions, ffn1_norm, ffn1_w_gate, ffn1_w_up, ffn1_w_down, mix_norm, ffn2_norm, ffn2_w_gate, ffn2_w_up, ffn2_w_down, ab_w_in, ab_mu_shift, rwkv_w0, rwkv_w_decay_up, rwkv_a0, rwkv_w_aaa_up, rwkv_w_gate_up, rwkv_k_k, rwkv_k_a, rwkv_r_k, rwkv_gn_gain, rwkv_gn_bias, sg_ln_gain, sg_ln_bias, sg_w_spatial, sg_b_spatial, ab_w_out, attn_w_qkv, attn_b_qkv, attn_q_norm, attn_k_norm, attn_sinks, attn_w_o, attn_b_o):
    bsz, t_len, d = x.shape
    depth = ffn1_norm.shape[0]
    a_width = rwkv_w0.shape[1]
    b_width = sg_ln_gain.shape[1]
    n_groups = sg_w_spatial.shape[1]
    assert a_width == b_width
    bf = lambda w: w.astype(BF16)

    xf = x.reshape(bsz * t_len, d)
    cos, sin = _rope_tables(positions.reshape(-1, 1))
    head_id = jnp.arange(2 * LANES) // HEAD_DIM
    ones_bd = (head_id[:, None] == head_id[None, :]).astype(BF16)
    pairs = LANES // HEAD_DIM
    ffn1 = [bf(w) for w in (ffn1_w_gate, ffn1_w_up, ffn1_w_down)]
    ffn2 = [bf(w) for w in (ffn2_w_gate, ffn2_w_up, ffn2_w_down)]
    w_in, w_out, w_qkv, w_o = bf(ab_w_in), bf(ab_w_out), bf(attn_w_qkv), bf(attn_w_o)

    for l in range(depth):
        i = l // 2
        xf = _ffn(xf, _row(ffn1_norm[l]), *(_layer(w, l) for w in ffn1))
        if l % 2 == 0:
            lora = rwkv_w_decay_up.shape[1]
            zeros = jnp.zeros((LANES - lora, a_width), F32)
            wdec = bf(jnp.concatenate([rwkv_w_decay_up[i], zeros], axis=0))
            waaa = bf(jnp.concatenate([zeros, rwkv_w_aaa_up[i]], axis=0))
            bsp = jnp.repeat(sg_b_spatial[i].T, b_width // n_groups, axis=1)
            r, lw, k, v, a, b, gate, bonus, yb = _mix_in(
                xf, _row(mix_norm[l]), _layer(w_in, i), _row(ab_mu_shift[i]), _row(rwkv_w0[i]), wdec,
                _row(rwkv_a0[i]), waaa, bf(rwkv_w_gate_up[i]), _row(rwkv_k_k[i]), _row(rwkv_k_a[i]),
                _row(rwkv_r_k[i]), _row(sg_ln_gain[i]), _row(sg_ln_bias[i]), sg_w_spatial[i], bsp, ones_bd,
                bsz=bsz)
            y = _scan(r, lw, k, v, a, b, bsz=bsz)
            xf = _mix_out_ffn(xf, y, bonus, gate, yb, _row(rwkv_gn_gain[i]), _row(rwkv_gn_bias[i]), ones_bd,
                              _layer(w_out, i, rows=a_width, row_block=0), _layer(w_out, i, rows=b_width, row_block=1),
                              _row(ffn2_norm[l]), *(_layer(w, l) for w in ffn2))
        else:
            xf = _attn(xf, cos, sin, _row(mix_norm[l]), _layer(w_qkv, i), _row(attn_b_qkv[i]),
                       _row(jnp.tile(attn_q_norm[i], pairs)), _row(jnp.tile(attn_k_norm[i], pairs)),
                       attn_sinks[i], _layer(w_o, i), _row(attn_b_o[i]), ones_bd, bsz=bsz)
            xf = _ffn(xf, _row(ffn2_norm[l]), *(_layer(w, l) for w in ffn2))
    return xf.reshape(bsz, t_len, d)
```
